```python
import math
import functools
import jax
import jax.numpy as jnp
from jax import lax
import numpy as np

D_MODEL = 1024
BATCH = 32
SEQ = 2048
DEPTH = 2
DEC_BATCH = 16
DEC_SEQ = 64
PAST_LEN = 1024

CHUNK = 64
Q_BLOCK = 128
PLE_DIM = 256
D_FF = 2816
LN_EPS = 1e-5
RMS_EPS = 1e-6
DEEPNORM_ALPHA = (2 * DEPTH) ** 0.25
DEEPNORM_BETA = (8 * DEPTH) ** -0.25
A_HEADS = 8
A_HD = 64
FORGET_BIAS_INIT = 3.0
B_HEADS = 4
B_HD = 64
B_ROT = B_HD // 4
ROPE_THETA = 500000.0
C_HEADS = 8
C_NOPE = 64
C_ROPE = 32
C_VD = 64
C_Q_LORA = 384
C_KV_LORA = 256
MLA_ROPE_THETA = 10000.0
N_BRANCH = 3
BRANCH_W = 512
MIX_SPLITS = (A_HEADS * A_HD, A_HEADS * A_HD, A_HEADS * A_HD, A_HEADS,
              2 * B_HEADS * B_HD, 2 * B_HEADS * B_HD, 2 * B_HEADS * B_HD,
              C_Q_LORA, C_KV_LORA, C_ROPE)
MIX_IN = sum(MIX_SPLITS)

kernel_name = 'hybrid_fox_diff_mla_streaming_step'


def _layer_norm(x, g, b):
    xf = x.astype(jnp.float32)
    mu = jnp.mean(xf, axis=-1, keepdims=True)
    var = jnp.mean(jnp.square(xf - mu), axis=-1, keepdims=True)
    y = (xf - mu) * lax.rsqrt(var + LN_EPS) * g.astype(jnp.float32) + b.astype(jnp.float32)
    return y.astype(x.dtype)


def _rms_norm(x, g):
    xf = x.astype(jnp.float32)
    y = xf * lax.rsqrt(jnp.mean(jnp.square(xf), axis=-1, keepdims=True) + RMS_EPS)
    return (y * g.astype(jnp.float32)).astype(x.dtype)


def _swiglu(x, w_in, w_out):
    gate, up = jnp.split(x @ w_in, 2, axis=-1)
    return (jax.nn.silu(gate) * up) @ w_out


def _rope(x, pos, theta):
    d = x.shape[-1]
    half = d // 2
    inv = jnp.exp(-math.log(theta) * jnp.arange(half, dtype=jnp.float32) * (2.0 / d))
    ang = pos.astype(jnp.float32)[:, None] * inv[None, :]
    shape = (1, pos.shape[0]) + (1,) * (x.ndim - 3) + (half,)
    cos = jnp.cos(ang).reshape(shape)
    sin = jnp.sin(ang).reshape(shape)
    xf = x.astype(jnp.float32)
    x1, x2 = xf[..., :half], xf[..., half:]
    return jnp.concatenate([x1 * cos - x2 * sin, x2 * cos + x1 * sin], axis=-1).astype(x.dtype)


def _partial_rope(x, pos):
    return jnp.concatenate([_rope(x[..., :B_ROT], pos, ROPE_THETA), x[..., B_ROT:]], axis=-1)


def _masked_softmax(logits, mask):
    return jax.nn.softmax(jnp.where(mask, logits.astype(jnp.float32), -jnp.inf), axis=-1)


def _chunk_visible(q_pos, k_pos):
    return (k_pos[None, :] // CHUNK) <= (q_pos[:, None] // CHUNK)


def _frame_visible(q_pos, k_pos):
    return k_pos[None, :] <= q_pos[:, None]


def _fox_core(q, fq, k, v, fk, q_pos, k_pos):
    s = jnp.einsum('bqhd,bkhd->bhqk', q, k).astype(jnp.float32) * (A_HD ** -0.5)
    s = s + jnp.swapaxes(fq, 1, 2)[..., :, None] - jnp.swapaxes(fk, 1, 2)[..., None, :]
    pr = _masked_softmax(s, _frame_visible(q_pos, k_pos))
    return jnp.einsum('bhqk,bkhd->bqhd', pr.astype(v.dtype), v)


def _diff_core(q, k, v, q_pos, k_pos, lam):
    s = jnp.einsum('bqhmd,bkhmd->bmhqk', q, k).astype(jnp.float32) * (B_HD ** -0.5)
    pr = _masked_softmax(s, _chunk_visible(q_pos, k_pos))
    a = pr[:, 0] - lam * pr[:, 1]
    return jnp.einsum('bhqk,bkhe->bqhe', a.astype(v.dtype), v)


def _mla_core(q_lat, q_pe, ckv, kpe, q_pos, k_pos):
    s = (jnp.einsum('bqhl,bkl->bhqk', q_lat, ckv) + jnp.einsum('bqhr,bkr->bhqk', q_pe, kpe))
    s = s.astype(jnp.float32) * ((C_NOPE + C_ROPE) ** -0.5)
    pr = _masked_softmax(s, _chunk_visible(q_pos, k_pos))
    return jnp.einsum('bhqk,bkl->bqhl', pr.astype(ckv.dtype), ckv)


def _attend(core, q_args, k_args, q_pos, k_pos, sweep):
    if not sweep:
        return core(*q_args, *k_args, q_pos, k_pos)
    n = q_args[0].shape[1]
    outs = []
    for s in range(0, n, Q_BLOCK):
        e = s + Q_BLOCK
        outs.append(core(*[a[:, s:e] for a in q_args], *[a[:, :e] for a in k_args], q_pos[s:e], k_pos[:e]))
    return jnp.concatenate(outs, axis=1)


def _token_mixing(h, pos, i, W, cache):
    bsz, t, _ = h.shape
    offsets = [int(o) for o in np.cumsum(MIX_SPLITS)[:-1]]
    qa, ka, va, fa, qb, kb, vb, dq, dkv, kr = jnp.split(h @ W['w_in_mix'][i], offsets, axis=-1)
    qa = qa.reshape(bsz, t, A_HEADS, A_HD)
    ka = ka.reshape(bsz, t, A_HEADS, A_HD)
    va = va.reshape(bsz, t, A_HEADS, A_HD)
    logf = jax.nn.log_sigmoid(fa.astype(jnp.float32) + W['b_forget'][i].astype(jnp.float32))
    qb = _partial_rope(qb.reshape(bsz, t, B_HEADS, 2, B_HD), pos)
    kb = _partial_rope(kb.reshape(bsz, t, B_HEADS, 2, B_HD), pos)
    vb = vb.reshape(bsz, t, B_HEADS, 2 * B_HD)
    lq1, lk1, lq2, lk2 = [W['diff_lambda'][i, j].astype(jnp.float32) for j in range(4)]
    lambda_init = 0.8 - 0.6 * math.exp(-0.3 * i)
    lam = jnp.exp(jnp.sum(lq1 * lk1)) - jnp.exp(jnp.sum(lq2 * lk2)) + lambda_init
    cq = (_rms_norm(dq, W['mla_q_norm_g'][i]) @ W['mla_w_uq'][i]).reshape(bsz, t, C_HEADS, C_NOPE + C_ROPE)
    w_ukv = W['mla_w_ukv'][i]
    w_uk, w_uv = w_ukv[..., :C_NOPE], w_ukv[..., C_NOPE:]
    q_lat = jnp.einsum('bthn,lhn->bthl', cq[..., :C_NOPE], w_uk)
    q_pe = _rope(cq[..., C_NOPE:], pos, MLA_ROPE_THETA)
    ckv = _rms_norm(dkv, W['mla_kv_norm_g'][i])
    kpe = _rope(kr[:, :, None, :], pos, MLA_ROPE_THETA)[:, :, 0, :]

    new_state = (ka, va, logf, kb, vb, ckv, kpe)
    if cache is None:
        sweep = True
        ka_all, va_all, kb_all, vb_all, ckv_all, kpe_all = ka, va, kb, vb, ckv, kpe
        fk = jnp.cumsum(logf, axis=1)
        fq = fk
        k_pos = pos
    else:
        sweep = False
        c_ka, c_va, c_logf, c_kb, c_vb, c_ckv, c_kpe = cache
        past = c_ka.shape[1]
        ka_all = jnp.concatenate([c_ka, ka], axis=1)
        va_all = jnp.concatenate([c_va, va], axis=1)
        kb_all = jnp.concatenate([c_kb, kb], axis=1)
        vb_all = jnp.concatenate([c_vb, vb], axis=1)
        ckv_all = jnp.concatenate([c_ckv, ckv], axis=1)
        kpe_all = jnp.concatenate([c_kpe, kpe], axis=1)
        fk = jnp.cumsum(jnp.concatenate([c_logf.astype(jnp.float32), logf], axis=1), axis=1)
        fq = fk[:, past:]
        k_pos = jnp.arange(past + t, dtype=jnp.int32)

    oa = _attend(_fox_core, (qa, fq), (ka_all, va_all, fk), pos, k_pos, sweep)
    ob = _attend(functools.partial(_diff_core, lam=lam), (qb,), (kb_all, vb_all), pos, k_pos, sweep)
    oc = _attend(_mla_core, (q_lat, q_pe), (ckv_all, kpe_all), pos, k_pos, sweep)

    oa = oa.reshape(bsz, t, BRANCH_W)
    ob = (_rms_norm(ob, W['diff_norm_g'][i]) * (1.0 - lambda_init)).reshape(bsz, t, BRANCH_W)
    oc = jnp.einsum('bthl,lhv->bthv', oc, w_uv).reshape(bsz, t, BRANCH_W)

    gates = jnp.split(jax.nn.sigmoid(h @ W['w_gate'][i] + W['b_gate'][i]), N_BRANCH, axis=-1)
    merged = (gates[0] * (oa @ W['w_branch'][i, 0])
              + gates[1] * (ob @ W['w_branch'][i, 1])
              + gates[2] * (oc @ W['w_branch'][i, 2]))
    return merged @ W['w_out'][i], new_state


def _layer(x, p, pos, i, W, cache):
    x = _layer_norm(DEEPNORM_ALPHA * x + 0.5 * _swiglu(x, W['ffn1_w_in'][i], W['ffn1_w_out'][i]),
                    W['ln_g'][i, 0], W['ln_b'][i, 0])
    mix, new_state = _token_mixing(x, pos, i, W, cache)
    x = _layer_norm(DEEPNORM_ALPHA * x + mix, W['ln_g'][i, 1], W['ln_b'][i, 1])
    x = _layer_norm(DEEPNORM_ALPHA * x + 0.5 * _swiglu(x, W['ffn2_w_in'][i], W['ffn2_w_out'][i]),
                    W['ln_g'][i, 2], W['ln_b'][i, 2])
    gate = jax.nn.sigmoid(x @ W['ple_w_gate'][i] + W['ple_b_gate'][i])
    x = _layer_norm(DEEPNORM_ALPHA * x + gate * (p @ W['ple_w_proj'][i]), W['ln_g'][i, 3], W['ln_b'][i, 3])
    return x, new_state


def setup_inputs(seed: int = 0) -> dict:
    key = jax.random.key(seed)
    k = jax.random.split(key, 32)
    D = D_MODEL

    def nrm(kk, shape, scale=1.0):
        return jax.random.normal(kk, shape, jnp.float32) * scale

    return {
        'x_prompt': nrm(k[0], (BATCH, SEQ, D)),
        'x_sample': nrm(k[1], (DEC_BATCH, DEC_SEQ, D)),
        'p_prompt': nrm(k[2], (DEPTH, BATCH, SEQ, PLE_DIM)),
        'p_sample': nrm(k[3], (DEPTH, DEC_BATCH, DEC_SEQ, PLE_DIM)),
        'cache_fox_k': nrm(k[4], (DEPTH, DEC_BATCH, PAST_LEN, A_HEADS, A_HD)),
        'cache_fox_v': nrm(k[5], (DEPTH, DEC_BATCH, PAST_LEN, A_HEADS, A_HD)),
        'cache_fox_logf': jax.nn.log_sigmoid(FORGET_BIAS_INIT + nrm(k[6], (DEPTH, DEC_BATCH, PAST_LEN, A_HEADS))),
        'cache_diff_k': nrm(k[7], (DEPTH, DEC_BATCH, PAST_LEN, B_HEADS, 2, B_HD)),
        'cache_diff_v': nrm(k[8], (DEPTH, DEC_BATCH, PAST_LEN, B_HEADS, 2 * B_HD)),
        'cache_mla_ckv': nrm(k[9], (DEPTH, DEC_BATCH, PAST_LEN, C_KV_LORA)),
        'cache_mla_kpe': nrm(k[10], (DEPTH, DEC_BATCH, PAST_LEN, C_ROPE)),
        'ffn1_w_in': nrm(k[11], (DEPTH, D, 2 * D_FF), D ** -0.5),
        'ffn1_w_out': nrm(k[12], (DEPTH, D_FF, D), DEEPNORM_BETA * D_FF ** -0.5),
        'ffn2_w_in': nrm(k[13], (DEPTH, D, 2 * D_FF), D ** -0.5),
        'ffn2_w_out': nrm(k[14], (DEPTH, D_FF, D), DEEPNORM_BETA * D_FF ** -0.5),
        'ln_g': 1.0 + nrm(k[15], (DEPTH, 4, D), 0.02),
        'ln_b': nrm(k[16], (DEPTH, 4, D), 0.02),
        'w_in_mix': nrm(k[17], (DEPTH, D, MIX_IN), D ** -0.5),
        'b_forget': FORGET_BIAS_INIT + nrm(k[18], (DEPTH, A_HEADS), 0.1),
        'diff_lambda': nrm(k[19], (DEPTH, 4, B_HD), 0.1),
        'diff_norm_g': 1.0 + nrm(k[20], (DEPTH, 2 * B_HD), 0.02),
        'mla_q_norm_g': 1.0 + nrm(k[21], (DEPTH, C_Q_LORA), 0.02),
        'mla_w_uq': nrm(k[22], (DEPTH, C_Q_LORA, C_HEADS * (C_NOPE + C_ROPE)), C_Q_LORA ** -0.5),
        'mla_kv_norm_g': 1.0 + nrm(k[23], (DEPTH, C_KV_LORA), 0.02),
        'mla_w_ukv': nrm(k[24], (DEPTH, C_KV_LORA, C_HEADS, C_NOPE + C_VD), C_KV_LORA ** -0.5),
        'w_branch': nrm(k[25], (DEPTH, N_BRANCH, BRANCH_W, D), BRANCH_W ** -0.5),
        'w_gate': nrm(k[26], (DEPTH, D, N_BRANCH * D), D ** -0.5),
        'b_gate': nrm(k[27], (DEPTH, N_BRANCH * D), 0.02),
        'w_out': nrm(k[28], (DEPTH, D, D), DEEPNORM_BETA * D ** -0.5),
        'ple_w_gate': nrm(k[29], (DEPTH, D, D), D ** -0.5),
        'ple_b_gate': nrm(k[30], (DEPTH, D), 0.02),
        'ple_w_proj': nrm(k[31], (DEPTH, PLE_DIM, D), DEEPNORM_BETA * PLE_DIM ** -0.5),
    }


def reference(x_prompt, x_sample, p_prompt, p_sample, cache_fox_k, cache_fox_v, cache_fox_logf,
              cache_diff_k, cache_diff_v, cache_mla_ckv, cache_mla_kpe,
              ffn1_w_in, ffn1_w_out, ffn2_w_in, ffn2_w_out, ln_g, ln_b, w_in_mix, b_forget,
              diff_lambda, diff_norm_g, mla_q_norm_g, mla_w_uq, mla_kv_norm_g, mla_w_ukv,
              w_branch, w_gate, b_gate, w_out, ple_w_gate, ple_b_gate, ple_w_proj):
    W = dict(ffn1_w_in=ffn1_w_in, ffn1_w_out=ffn1_w_out, ffn2_w_in=ffn2_w_in, ffn2_w_out=ffn2_w_out,
             ln_g=ln_g, ln_b=ln_b, w_in_mix=w_in_mix, b_forget=b_forget, diff_lambda=diff_lambda,
             diff_norm_g=diff_norm_g, mla_q_norm_g=mla_q_norm_g, mla_w_uq=mla_w_uq,
             mla_kv_norm_g=mla_kv_norm_g, mla_w_ukv=mla_w_ukv, w_branch=w_branch, w_gate=w_gate,
             b_gate=b_gate, w_out=w_out, ple_w_gate=ple_w_gate, ple_b_gate=ple_b_gate,
             ple_w_proj=ple_w_proj)
    pos_p = jnp.arange(x_prompt.shape[1], dtype=jnp.int32)
    past = cache_fox_k.shape[2]
    pos_s = past + jnp.arange(x_sample.shape[1], dtype=jnp.int32)

    xp, xs = x_prompt, x_sample
    states_p, states_s = [], []
    for i in range(DEPTH):
        xp, st_p = _layer(xp, p_prompt[i], pos_p, i, W, None)
        states_p.append(st_p)
        cache_i = (cache_fox_k[i], cache_fox_v[i], cache_fox_logf[i], cache_diff_k[i], cache_diff_v[i],
                   cache_mla_ckv[i], cache_mla_kpe[i])
        xs, st_s = _layer(xs, p_sample[i], pos_s, i, W, cache_i)
        states_s.append(st_s)

    fox_k_p, fox_v_p, fox_logf_p, diff_k_p, diff_v_p, mla_ckv_p, mla_kpe_p = [
        jnp.stack([st[j] for st in states_p], axis=0) for j in range(7)]
    fox_k_s, fox_v_s, fox_logf_s, diff_k_s, diff_v_s, mla_ckv_s, mla_kpe_s = [
        jnp.stack([st[j] for st in states_s], axis=0) for j in range(7)]
    return (xp, xs, fox_k_p, fox_k_s, fox_v_p, fox_v_s, fox_logf_p, fox_logf_s,
            diff_k_p, diff_k_s, diff_v_p, diff_v_s, mla_ckv_p, mla_ckv_s, mla_kpe_p, mla_kpe_s)
```

```python
import functools
import math

import jax
import jax.numpy as jnp
from jax import lax
from jax.experimental import pallas as pl
from jax.experimental.pallas import tpu as pltpu

F32 = jnp.float32
BF16 = jnp.bfloat16

D_MODEL = 1024
DEPTH = 2
CHUNK = 64
PLE_DIM = 256
D_FF = 2816
LN_EPS = 1e-5
RMS_EPS = 1e-6
ALPHA = (2 * DEPTH) ** 0.25
A_HEADS, A_HD = 8, 64
B_HEADS, B_HD = 4, 64
B_ROT = B_HD // 4
ROPE_THETA = 500000.0
C_HEADS, C_NOPE, C_ROPE, C_VD = 8, 64, 32, 64
C_Q_LORA, C_KV_LORA = 384, 256
MLA_ROPE_THETA = 10000.0
BRANCH_W = 512
MIX_SPLITS = (512, 512, 512, 8, 512, 512, 512, C_Q_LORA, C_KV_LORA, C_ROPE)

LANES = 128
FF_CHUNK = 256
N_FF_CHUNKS = D_FF // FF_CHUNK
O_QA, O_KA, O_VA, O_QB, O_KB, O_VB, O_DQ, O_DKV, O_FA, O_KR, MIX_PAD = (
    0, 512, 1024, 1536, 2048, 2560, 3072, 3456, 3712, 3840, 3968)
NEG = -1e30
VMEM_LIMIT = 56 * 1024 * 1024


def _cparams(n_axes):
    return pltpu.CompilerParams(dimension_semantics=("parallel",) * n_axes, vmem_limit_bytes=VMEM_LIMIT)


def _const_spec(shape, layer=None):
    if layer is None:
        return pl.BlockSpec(shape, lambda *_: (0,) * len(shape), pipeline_mode=pl.Buffered(1))
    return pl.BlockSpec((None,) + tuple(shape), lambda *_: (layer,) + (0,) * len(shape),
                        pipeline_mode=pl.Buffered(1))


def _layer_norm(z, g, b):
    mu = jnp.mean(z, axis=-1, keepdims=True)
    zc = z - mu
    var = jnp.mean(zc * zc, axis=-1, keepdims=True)
    return zc * lax.rsqrt(var + LN_EPS) * g + b


def _rms_norm(x, g):
    return x * lax.rsqrt(jnp.mean(x * x, axis=-1, keepdims=True) + RMS_EPS) * g


def _dot(a, b):
    return jnp.dot(a, b, preferred_element_type=F32)


def _ffn_kernel(*refs, ple):
    if ple:
        (x_ref, win_ref, wout_ref, g_ref, b_ref, p_ref, wpg_ref, bpg_ref, wpp_ref, g2_ref, b2_ref,
         o_ref, act_ref) = refs
    else:
        x_ref, win_ref, wout_ref, g_ref, b_ref, o_ref, act_ref = refs
    x = x_ref[...]
    xb = x.astype(BF16)
    for c in range(N_FF_CHUNKS):
        h = _dot(xb, win_ref[c])
        gate, up = h[:, :FF_CHUNK], h[:, FF_CHUNK:]
        act_ref[:, c * FF_CHUNK:(c + 1) * FF_CHUNK] = (gate * jax.nn.sigmoid(gate) * up).astype(BF16)
    y = _dot(act_ref[...], wout_ref[...])
    x1 = _layer_norm(ALPHA * x + 0.5 * y, g_ref[...], b_ref[...])
    if ple:
        gate = jax.nn.sigmoid(_dot(x1.astype(BF16), wpg_ref[...]) + bpg_ref[...])
        emb = _dot(p_ref[...].astype(BF16), wpp_ref[...])
        x1 = _layer_norm(ALPHA * x1 + gate * emb, g2_ref[...], b2_ref[...])
    o_ref[...] = x1


def _ffn_ln(x, w, layer, which, p=None, tm=512):
    n, d = x.shape
    tm = min(tm, n)
    ple = p is not None
    row = lambda w_: pl.BlockSpec((tm, w_), lambda i: (i, 0))
    in_specs = [row(d),
                _const_spec((N_FF_CHUNKS, d, 2 * FF_CHUNK), layer),
                _const_spec((D_FF, d), layer),
                _const_spec((1, d)), _const_spec((1, d))]
    ln_idx = 0 if which == 1 else 2
    args = [x, w[f"ffn{which}_w_in"], w[f"ffn{which}_w_out"], w["ln_g"][layer][ln_idx], w["ln_b"][layer][ln_idx]]
    if ple:
        in_specs += [row(PLE_DIM), _const_spec((d, d), layer), _const_spec((1, d)),
                     _const_spec((PLE_DIM, d), layer), _const_spec((1, d)), _const_spec((1, d))]
        args += [p, w["ple_w_gate"], w["ple_b_gate"][layer], w["ple_w_proj"],
                 w["ln_g"][layer][3], w["ln_b"][layer][3]]
    return pl.pallas_call(
        functools.partial(_ffn_kernel, ple=ple),
        grid=(n // tm,),
        in_specs=in_specs,
        out_specs=row(d),
        out_shape=jax.ShapeDtypeStruct((n, d), F32),
        scratch_shapes=[pltpu.VMEM((tm, D_FF), BF16)],
        compiler_params=_cparams(1),
        name=f"ffn{which}_ln",
    )(*args)


def _rope(x, tab_ref, shift):
    return (x * tab_ref[0]
            + pltpu.roll(x, LANES - shift, axis=1) * tab_ref[1]
            + pltpu.roll(x, shift, axis=1) * tab_ref[2])


def _mla_kv(ckv_b, wukv_ref, kpe_wide, km_o, vm_o):
    kv = _dot(ckv_b, wukv_ref[...])
    for h in range(C_HEADS):
        km_o[:, h * LANES:(h + 1) * LANES] = (kv[:, h * LANES:(h + 1) * LANES] + kpe_wide).astype(BF16)
    vm_o[...] = kv[:, C_HEADS * LANES:].astype(BF16)


def _mix_kernel(x_ref, w_ref, bf_ref, gq_ref, gkv_ref, wuq_ref, wukv_ref, tdiff_ref, tmla_ref, tkr_ref,
                qa_o, ka_o, kab_o, va_o, vab_o, logf_o, qb_o, kb_o, kbb_o, vb_o, vbb_o,
                ckv_o, kpe_o, qm_o, km_o, vm_o):
    xb = x_ref[...].astype(BF16)

    def seg(a, b):
        return _dot(xb, w_ref[:, a:b])

    qa_o[...] = seg(O_QA, O_KA).astype(BF16)
    ka = seg(O_KA, O_VA)
    ka_o[...] = ka
    kab_o[...] = ka.astype(BF16)
    va = seg(O_VA, O_QB)
    va_o[...] = va
    vab_o[...] = va.astype(BF16)

    qb = seg(O_QB, O_KB)
    kb = seg(O_KB, O_VB)
    for g in range(4):
        sl = slice(g * LANES, (g + 1) * LANES)
        qb_o[:, sl] = _rope(qb[:, sl], tdiff_ref, B_ROT // 2).astype(BF16)
        kr_ = _rope(kb[:, sl], tdiff_ref, B_ROT // 2)
        kb_o[:, sl] = kr_
        kbb_o[:, sl] = kr_.astype(BF16)
    vb = seg(O_VB, O_DQ)
    vb_o[...] = vb
    vbb_o[...] = vb.astype(BF16)

    cq = _rms_norm(seg(O_DQ, O_DKV), gq_ref[...]).astype(BF16)
    qm = _dot(cq, wuq_ref[...]) * ((C_NOPE + C_ROPE) ** -0.5)
    for h in range(C_HEADS):
        sl = slice(h * LANES, (h + 1) * LANES)
        qm_o[:, sl] = _rope(qm[:, sl], tmla_ref, C_ROPE // 2).astype(BF16)

    ckv = _rms_norm(seg(O_DKV, O_FA), gkv_ref[...])
    ckv_o[...] = ckv

    fa = seg(O_FA, O_KR)[:, :A_HEADS] + bf_ref[...]
    logf_o[...] = jnp.minimum(fa, 0.0) - jnp.log1p(jnp.exp(-jnp.abs(fa)))

    kr = _rope(seg(O_KR, MIX_PAD), tkr_ref, C_ROPE // 2)
    kpe_o[...] = kr[:, :C_ROPE]
    lane = lax.broadcasted_iota(jnp.int32, kr.shape, 1)
    _mla_kv(ckv.astype(BF16), wukv_ref, jnp.where(lane >= C_NOPE, kr, 0.0), km_o, vm_o)


def _mix_proj(x, w, layer, tabs, tm=512):
    n, d = x.shape
    tm = min(tm, n)
    tdiff, tmla, tkr = tabs
    n_tab = tdiff.shape[1] // tm
    row = lambda w_: pl.BlockSpec((tm, w_), lambda i: (i, 0))
    tab = pl.BlockSpec((3, tm, LANES), lambda i: (0, i % n_tab, 0))
    out_widths = [(512, BF16), (512, F32), (512, BF16), (512, F32), (512, BF16), (A_HEADS, F32),
                  (512, BF16), (512, F32), (512, BF16), (512, F32), (512, BF16),
                  (C_KV_LORA, F32), (C_ROPE, F32), (C_HEADS * LANES, BF16), (C_HEADS * LANES, BF16), (512, BF16)]
    return pl.pallas_call(
        _mix_kernel,
        grid=(n // tm,),
        in_specs=[row(d), _const_spec((d, MIX_PAD), layer), _const_spec((1, A_HEADS)),
                  _const_spec((1, C_Q_LORA)), _const_spec((1, C_KV_LORA)),
                  _const_spec((C_Q_LORA, C_HEADS * LANES), layer),
                  _const_spec((C_KV_LORA, C_HEADS * LANES + 512), layer), tab, tab, tab],
        out_specs=[row(w_) for w_, _ in out_widths],
        out_shape=[jax.ShapeDtypeStruct((n, w_), dt) for w_, dt in out_widths],
        compiler_params=_cparams(1),
        name="mix_proj",
    )(x, w["w_mix"], w["b_forget"][layer], w["mla_q_norm_g"][layer], w["mla_kv_norm_g"][layer],
      w["mla_w_uq"], w["mla_w_ukv"], tdiff, tmla, tkr)


def _mla_cache_kernel(ckv_ref, kpe_ref, wukv_ref, km_o, vm_o):
    _mla_kv(ckv_ref[...].astype(BF16), wukv_ref, kpe_ref[...], km_o, vm_o)


def _mla_cache_proj(ckv, kpe_wide, w, layer, tm=512):
    n = ckv.shape[0]
    row = lambda w_: pl.BlockSpec((tm, w_), lambda i: (i, 0))
    return pl.pallas_call(
        _mla_cache_kernel,
        grid=(n // tm,),
        in_specs=[row(C_KV_LORA), row(LANES), _const_spec((C_KV_LORA, C_HEADS * LANES + 512), layer)],
        out_specs=[row(C_HEADS * LANES), row(512)],
        out_shape=[jax.ShapeDtypeStruct((n, C_HEADS * LANES), BF16), jax.ShapeDtypeStruct((n, 512), BF16)],
        compiler_params=_cparams(1),
        name="mla_cache_proj",
    )(ckv, kpe_wide, w["mla_w_ukv"])


def _cumsum_kernel(x_ref, o_ref):
    x = x_ref[0]
    lane = lax.broadcasted_iota(jnp.int32, x.shape, 1)
    s = 1
    while s < x.shape[1]:
        x = x + jnp.where(lane >= s, pltpu.roll(x, s, axis=1), 0.0)
        s *= 2
    o_ref[0] = x


def _cumsum_time(x):
    b, h, t = x.shape
    spec = pl.BlockSpec((1, h, t), lambda i: (i, 0, 0))
    return pl.pallas_call(_cumsum_kernel, grid=(b,), in_specs=[spec], out_specs=spec,
                          out_shape=jax.ShapeDtypeStruct(x.shape, F32), compiler_params=_cparams(1),
                          name="cumsum_time")(x)


def _attn_kernel(*refs, mode, tq, tk, q_off, lam_init):
    if mode == "fox":
        q_ref, k_ref, v_ref, fq_ref, fk_ref, o_ref, m_s, l_s, acc_s = refs
    elif mode == "diff":
        q_ref, k_ref, v_ref, lam_ref, g_ref, o_ref, m_s, l_s, acc_s = refs
    else:
        q_ref, k_ref, v_ref, o_ref, m_s, l_s, acc_s = refs
    qi = pl.program_id(2)
    lo = lax.broadcasted_iota(jnp.int32, (tq, LANES), 1) < 64
    if mode == "mla":
        qs = [q_ref[0, :, :LANES], q_ref[0, :, LANES:]]
        ksl = [slice(0, LANES), slice(LANES, 2 * LANES)]
    else:
        qf = q_ref[0].astype(F32)
        qs = [jnp.where(lo, qf, 0.0).astype(BF16), jnp.where(lo, 0.0, qf).astype(BF16)]
        ksl = [slice(0, LANES), slice(0, LANES)]

    m_s[...] = jnp.full(m_s.shape, NEG, F32)
    l_s[...] = jnp.zeros(l_s.shape, F32)
    acc_s[...] = jnp.zeros(acc_s.shape, F32)

    q0 = q_off + qi * tq
    n_full = q0 // tk

    def step(j, masked):
        ks = pl.multiple_of(j * tk, tk)
        kb = k_ref[0, pl.ds(ks, tk), :]
        vb = v_ref[0, pl.ds(ks, tk), :]
        if masked:
            qpos = q0 + lax.broadcasted_iota(jnp.int32, (tq, tk), 0)
            kpos = ks + lax.broadcasted_iota(jnp.int32, (tq, tk), 1)
            if mode == "fox":
                vis = kpos <= qpos
            else:
                shift = CHUNK.bit_length() - 1
                vis = jnp.right_shift(kpos, shift) <= jnp.right_shift(qpos, shift)
        for mm in range(2):
            s = lax.dot_general(qs[mm], kb[:, ksl[mm]], (((1,), (1,)), ((), ())), preferred_element_type=F32)
            if mode == "fox":
                s = s + (fq_ref[0, :, mm:mm + 1] - fk_ref[0, j, mm:mm + 1, :])
            if masked:
                s = jnp.where(vis, s, NEG)
            m_old = m_s[mm]
            m_new = jnp.maximum(m_old, jnp.max(s, axis=-1, keepdims=True))
            alpha = jnp.exp(m_old - m_new)
            p = jnp.exp(s - m_new)
            l_s[mm] = alpha * l_s[mm] + jnp.sum(p, axis=-1, keepdims=True)
            acc_s[mm] = alpha * acc_s[mm] + _dot(p.astype(BF16), vb)
            m_s[mm] = m_new

    def full_step(j, carry):
        step(j, False)
        return carry

    lax.fori_loop(0, n_full, full_step, 0)
    step(n_full, True)

    o0 = acc_s[0] / l_s[0]
    o1 = acc_s[1] / l_s[1]
    if mode == "diff":
        lam = (jnp.exp(jnp.sum(lam_ref[0:1, :] * lam_ref[1:2, :], axis=-1, keepdims=True))
               - jnp.exp(jnp.sum(lam_ref[2:3, :] * lam_ref[3:4, :], axis=-1, keepdims=True)) + lam_init)
        out = _rms_norm(o0 - lam * o1, g_ref[...]) * (1.0 - lam_init)
    else:
        out = jnp.where(lo, o0, o1)
    o_ref[0] = out.astype(BF16)


def _attention(mode, q, k, v, extra, *, tq, tk, q_off, lam_init=0.0):
    b, t_q, _ = q.shape
    t_k = k.shape[1]
    assert t_q % tq == 0 and t_k % tk == 0
    assert all((q_off + i * tq) // tk == (q_off + (i + 1) * tq - 1) // tk for i in range(t_q // tq))
    qw = 2 * LANES if mode == "mla" else LANES
    in_specs = [pl.BlockSpec((1, tq, qw), lambda bi, g, qi: (bi, qi, g)),
                pl.BlockSpec((1, t_k, qw), lambda bi, g, qi: (bi, 0, g)),
                pl.BlockSpec((1, t_k, LANES), lambda bi, g, qi: (bi, 0, g))]
    if mode == "fox":
        in_specs += [pl.BlockSpec((1, None, tq, 2), lambda bi, g, qi: (bi, g, qi, 0)),
                     pl.BlockSpec((1, None, t_k // tk, 2, tk), lambda bi, g, qi: (bi, g, 0, 0, 0))]
    elif mode == "diff":
        in_specs += [_const_spec((4, B_HD)), _const_spec((1, 2 * B_HD))]
    return pl.pallas_call(
        functools.partial(_attn_kernel, mode=mode, tq=tq, tk=tk, q_off=q_off, lam_init=lam_init),
        grid=(b, 4, t_q // tq),
        in_specs=in_specs,
        out_specs=pl.BlockSpec((1, tq, LANES), lambda bi, g, qi: (bi, qi, g)),
        out_shape=jax.ShapeDtypeStruct((b, t_q, 4 * LANES), BF16),
        scratch_shapes=[pltpu.VMEM((2, tq, 1), F32), pltpu.VMEM((2, tq, 1), F32), pltpu.VMEM((2, tq, LANES), F32)],
        compiler_params=_cparams(3),
        name=f"attn_{mode}",
    )(q, k, v, *extra)


def _merge_kernel(h_ref, oa_ref, ob_ref, oc_ref, wg_ref, bg_ref, wb_ref, wo_ref, g_ref, b_ref, o_ref):
    h = h_ref[...]
    hb = h.astype(BF16)
    d = h.shape[1]
    merged = None
    for br, o_br in enumerate((oa_ref, ob_ref, oc_ref)):
        gate = jax.nn.sigmoid(_dot(hb, wg_ref[:, br * d:(br + 1) * d]) + bg_ref[:, br * d:(br + 1) * d])
        term = gate * _dot(o_br[...], wb_ref[br])
        merged = term if merged is None else merged + term
    mix = _dot(merged.astype(BF16), wo_ref[...])
    o_ref[...] = _layer_norm(ALPHA * h + mix, g_ref[...], b_ref[...])


def _merge(h, oa, ob, oc, w, layer, tm=512):
    n, d = h.shape
    tm = min(tm, n)
    row = lambda w_: pl.BlockSpec((tm, w_), lambda i: (i, 0))
    return pl.pallas_call(
        _merge_kernel,
        grid=(n // tm,),
        in_specs=[row(d), row(BRANCH_W), row(BRANCH_W), row(BRANCH_W),
                  _const_spec((d, 3 * d), layer), _const_spec((1, 3 * d)),
                  _const_spec((3, BRANCH_W, d), layer), _const_spec((d, d), layer),
                  _const_spec((1, d)), _const_spec((1, d))],
        out_specs=row(d),
        out_shape=jax.ShapeDtypeStruct((n, d), F32),
        compiler_params=_cparams(1),
        name="merge",
    )(h, oa, ob, oc, w["w_gate"], w["b_gate"][layer], w["w_branch"], w["w_out"],
      w["ln_g"][layer][1], w["ln_b"][layer][1])


def _rope_table(pos, groups, rows):
    t = pos.shape[0]
    one, zero = jnp.ones((t, 1), F32), jnp.zeros((t, 1), F32)
    c_cols, s1_cols, s2_cols = [], [], []
    lane = 0
    for start, half, theta in groups:
        inv = jnp.exp(-math.log(theta) * jnp.arange(half, dtype=F32) * (2.0 / (2 * half)))
        ang = pos.astype(F32)[:, None] * inv[None, :]
        cos, sin = jnp.cos(ang), jnp.sin(ang)
        gap = start - lane
        c_cols += [jnp.tile(one, (1, gap)), cos, cos]
        s1_cols += [jnp.tile(zero, (1, gap)), -sin, jnp.zeros_like(sin)]
        s2_cols += [jnp.tile(zero, (1, gap)), jnp.zeros_like(sin), sin]
        lane = start + 2 * half
    gap = LANES - lane
    tabs = [jnp.concatenate(cols + [jnp.tile(fill, (1, gap))], axis=1)
            for cols, fill in ((c_cols, one), (s1_cols, zero), (s2_cols, zero))]
    tab = jnp.stack(tabs, axis=0)
    return jnp.tile(tab, (1, rows // t, 1)) if rows > t else tab


def _rope_tables(pos, rows):
    h_b, h_c = B_ROT // 2, C_ROPE // 2
    return (_rope_table(pos, [(0, h_b, ROPE_THETA), (B_HD, h_b, ROPE_THETA)], rows),
            _rope_table(pos, [(C_NOPE, h_c, MLA_ROPE_THETA)], rows),
            _rope_table(pos, [(0, h_c, MLA_ROPE_THETA), (C_NOPE, h_c, MLA_ROPE_THETA)], rows))


def _prepare_weights(ffn1_w_in, ffn1_w_out, ffn2_w_in, ffn2_w_out, ln_g, ln_b, w_in_mix, b_forget,
                     diff_lambda, diff_norm_g, mla_q_norm_g, mla_w_uq, mla_kv_norm_g, mla_w_ukv,
                     w_branch, w_gate, b_gate, w_out, ple_w_gate, ple_b_gate, ple_w_proj):
    depth, d = w_in_mix.shape[0], w_in_mix.shape[1]

    def ffn_in(w_in):
        gate = w_in[..., :D_FF].reshape(depth, d, N_FF_CHUNKS, FF_CHUNK)
        up = w_in[..., D_FF:].reshape(depth, d, N_FF_CHUNKS, FF_CHUNK)
        return jnp.concatenate([gate, up], axis=-1).transpose(0, 2, 1, 3).astype(BF16)

    offs = [sum(MIX_SPLITS[:i]) for i in range(1, len(MIX_SPLITS))]
    qa, ka, va, fa, qb, kb, vb, dq, dkv, kr = jnp.split(w_in_mix, offs, axis=-1)
    z = lambda n_: jnp.zeros((depth, d, n_), F32)
    w_mix = jnp.concatenate([qa * (A_HD ** -0.5), ka, va, qb * (B_HD ** -0.5), kb, vb, dq, dkv,
                             fa, z(LANES - A_HEADS), kr, z(C_NOPE - C_ROPE), kr, z(LANES - C_NOPE - C_ROPE)],
                            axis=-1).astype(BF16)
    w_uq = mla_w_uq.reshape(depth, C_Q_LORA, C_HEADS, C_NOPE + C_ROPE)
    w_uq = jnp.pad(w_uq, ((0, 0), (0, 0), (0, 0), (0, LANES - C_NOPE - C_ROPE)))
    w_uq = w_uq.reshape(depth, C_Q_LORA, C_HEADS * LANES).astype(BF16)
    w_uk = jnp.pad(mla_w_ukv[..., :C_NOPE], ((0, 0), (0, 0), (0, 0), (0, LANES - C_NOPE)))
    w_uv = mla_w_ukv[..., C_NOPE:]
    w_ukv = jnp.concatenate([w_uk.reshape(depth, C_KV_LORA, C_HEADS * LANES),
                             w_uv.reshape(depth, C_KV_LORA, C_HEADS * C_VD)], axis=-1).astype(BF16)
    vec = lambda a: a[:, None, :]
    return dict(
        ffn1_w_in=ffn_in(ffn1_w_in), ffn1_w_out=ffn1_w_out.astype(BF16),
        ffn2_w_in=ffn_in(ffn2_w_in), ffn2_w_out=ffn2_w_out.astype(BF16),
        ln_g=ln_g[:, :, None, :], ln_b=ln_b[:, :, None, :],
        w_mix=w_mix, b_forget=vec(b_forget), diff_lambda=diff_lambda, diff_norm_g=vec(diff_norm_g),
        mla_q_norm_g=vec(mla_q_norm_g), mla_kv_norm_g=vec(mla_kv_norm_g), mla_w_uq=w_uq, mla_w_ukv=w_ukv,
        w_branch=w_branch.astype(BF16), w_gate=w_gate.astype(BF16), b_gate=vec(b_gate),
        w_out=w_out.astype(BF16), ple_w_gate=ple_w_gate.astype(BF16), ple_b_gate=vec(ple_b_gate),
        ple_w_proj=ple_w_proj.astype(BF16))


def _layer_step(x, p, layer, w, tabs, cache, past):
    b, t, d = x.shape
    n = b * t
    lam_init = 0.8 - 0.6 * math.exp(-0.3 * layer)
    x1 = _ffn_ln(x.reshape(n, d), w, layer, 1)
    (qa, ka, kab, va, vab, logf, qb, kb, kbb, vb, vbb, ckv, kpe, qm, km, vm) = _mix_proj(x1, w, layer, tabs)
    r3 = lambda a: a.reshape(b, t, a.shape[-1])
    new_state = (r3(ka).reshape(b, t, A_HEADS, A_HD), r3(va).reshape(b, t, A_HEADS, A_HD), r3(logf),
                 r3(kb).reshape(b, t, B_HEADS, 2, B_HD), r3(vb).reshape(b, t, B_HEADS, 2 * B_HD), r3(ckv), r3(kpe))
    qa, kab, vab, qb, kbb, vbb, qm, km, vm = map(r3, (qa, kab, vab, qb, kbb, vbb, qm, km, vm))
    logf_t = r3(logf).transpose(0, 2, 1)
    if cache is None:
        tq = tk = min(256, t)
        q_off = 0
        cum = _cumsum_time(logf_t)
        fq_src = cum
    else:
        c_ka, c_va, c_logf, c_kb, c_vb, c_ckv, c_kpe = cache
        t_k = -(-(past + t) // LANES) * LANES
        pad = t_k - past - t
        tq, tk, q_off = t, t_k, past

        def with_cache(c, new):
            c = c.reshape(b, past, -1).astype(BF16)
            return jnp.concatenate([c, new, jnp.zeros((b, pad, new.shape[-1]), BF16)], axis=1)

        kab, vab, kbb, vbb = (with_cache(c, a) for c, a in ((c_ka, kab), (c_va, vab), (c_kb, kbb), (c_vb, vbb)))
        kpe_wide = jnp.pad(c_kpe.reshape(b * past, C_ROPE), ((0, 0), (C_NOPE, LANES - C_NOPE - C_ROPE)))
        km_c, vm_c = _mla_cache_proj(c_ckv.reshape(b * past, C_KV_LORA), kpe_wide, w, layer)
        km = jnp.concatenate([km_c.reshape(b, past, -1), km, jnp.zeros((b, pad, km.shape[-1]), BF16)], axis=1)
        vm = jnp.concatenate([vm_c.reshape(b, past, -1), vm, jnp.zeros((b, pad, vm.shape[-1]), BF16)], axis=1)
        logf_all = jnp.concatenate([c_logf.astype(F32).transpose(0, 2, 1), logf_t,
                                    jnp.zeros((b, A_HEADS, pad), F32)], axis=2)
        cum = _cumsum_time(logf_all)
        fq_src = cum[:, :, past:past + t]
    fq = fq_src.reshape(b, 4, 2, t).transpose(0, 1, 3, 2)
    fk = cum.reshape(b, 4, 2, cum.shape[2] // tk, tk).transpose(0, 1, 3, 2, 4)
    oa = _attention("fox", qa, kab, vab, (fq, fk), tq=tq, tk=tk, q_off=q_off)
    ob = _attention("diff", qb, kbb, vbb, (w["diff_lambda"][layer], w["diff_norm_g"][layer]),
                    tq=tq, tk=tk, q_off=q_off, lam_init=lam_init)
    oc = _attention("mla", qm, km, vm, (), tq=tq, tk=tk, q_off=q_off)
    f2 = lambda a: a.reshape(n, a.shape[-1])
    x2 = _merge(x1, f2(oa), f2(ob), f2(oc), w, layer)
    x3 = _ffn_ln(x2, w, layer, 2, p=p.reshape(n, PLE_DIM))
    return x3.reshape(b, t, d), new_state


def kernel(x_prompt, x_sample, p_prompt, p_sample, cache_fox_k, cache_fox_v, cache_fox_logf, cache_diff_k, cache_diff_v, cache_mla_ckv, cache_mla_kpe, ffn1_w_in, ffn1_w_out, ffn2_w_in, ffn2_w_out, ln_g, ln_b, w_in_mix, b_forget, diff_lambda, diff_norm_g, mla_q_norm_g, mla_w_uq, mla_kv_norm_g, mla_w_ukv, w_branch, w_gate, b_gate, w_out, ple_w_gate, ple_b_gate, ple_w_proj):
    w = _prepare_weights(ffn1_w_in, ffn1_w_out, ffn2_w_in, ffn2_w_out, ln_g, ln_b, w_in_mix, b_forget,
                         diff_lambda, diff_norm_g, mla_q_norm_g, mla_w_uq, mla_kv_norm_g, mla_w_ukv,
                         w_branch, w_gate, b_gate, w_out, ple_w_gate, ple_b_gate, ple_w_proj)
    depth = w_in_mix.shape[0]
    t_p, t_s = x_prompt.shape[1], x_sample.shape[1]
    past = cache_fox_k.shape[2]
    n_s = x_sample.shape[0] * t_s
    tabs_p = _rope_tables(jnp.arange(t_p, dtype=jnp.int32), t_p)
    tabs_s = _rope_tables(past + jnp.arange(t_s, dtype=jnp.int32), min(512, n_s))

    xp, xs = x_prompt, x_sample
    states_p, states_s = [], []
    for i in range(depth):
        xp, st_p = _layer_step(xp, p_prompt[i], i, w, tabs_p, None, 0)
        states_p.append(st_p)
        cache_i = (cache_fox_k[i], cache_fox_v[i], cache_fox_logf[i], cache_diff_k[i], cache_diff_v[i],
                   cache_mla_ckv[i], cache_mla_kpe[i])
        xs, st_s = _layer_step(xs, p_sample[i], i, w, tabs_s, cache_i, past)
        states_s.append(st_s)
    sp = [jnp.stack([st[j] for st in states_p], axis=0) for j in range(7)]
    ss = [jnp.stack([st[j] for st in states_s], axis=0) for j in range(7)]
    return (xp, xs, sp[0], ss[0], sp[1], ss[1], sp[2], ss[2], sp[3], ss[3], sp[4], ss[4],
            sp[5], ss[5], sp[6], ss[6])
```

```python
import functools
import math

import jax
import jax.numpy as jnp
from jax import lax
from jax.experimental import pallas as pl
from jax.experimental.pallas import tpu as pltpu

F32 = jnp.float32
BF16 = jnp.bfloat16

D_MODEL = 1024
DEPTH = 2
CHUNK = 64
PLE_DIM = 256
D_FF = 2816
LN_EPS = 1e-5
RMS_EPS = 1e-6
ALPHA = (2 * DEPTH) ** 0.25
A_HEADS, A_HD = 8, 64
B_HEADS, B_HD = 4, 64
B_ROT = B_HD // 4
ROPE_THETA = 500000.0
C_HEADS, C_NOPE, C_ROPE, C_VD = 8, 64, 32, 64
C_Q_LORA, C_KV_LORA = 384, 256
MLA_ROPE_THETA = 10000.0
BRANCH_W = 512
MIX_SPLITS = (512, 512, 512, 8, 512, 512, 512, C_Q_LORA, C_KV_LORA, C_ROPE)

LANES = 128
FF_CHUNK = 256
N_FF_CHUNKS = D_FF // FF_CHUNK
O_QA, O_KA, O_VA, O_QB, O_KB, O_VB, O_DQ, O_DKV, O_FA, O_KR, MIX_PAD = (
    0, 512, 1024, 1536, 2048, 2560, 3072, 3456, 3712, 3840, 3968)
NEG = -1e30
LOG2E = math.log2(math.e)
VMEM_LIMIT = 56 * 1024 * 1024


def _cparams(n_axes):
    return pltpu.CompilerParams(dimension_semantics=("parallel",) * n_axes, vmem_limit_bytes=VMEM_LIMIT)


def _const_spec(shape, layer=None):
    if layer is None:
        return pl.BlockSpec(shape, lambda *_: (0,) * len(shape), pipeline_mode=pl.Buffered(1))
    return pl.BlockSpec((None,) + tuple(shape), lambda *_: (layer,) + (0,) * len(shape),
                        pipeline_mode=pl.Buffered(1))


def _layer_norm(z, g, b):
    mu = jnp.mean(z, axis=-1, keepdims=True)
    zc = z - mu
    var = jnp.mean(zc * zc, axis=-1, keepdims=True)
    return zc * lax.rsqrt(var + LN_EPS) * g + b


def _rms_norm(x, g):
    return x * lax.rsqrt(jnp.mean(x * x, axis=-1, keepdims=True) + RMS_EPS) * g


def _dot(a, b):
    return jnp.dot(a, b, preferred_element_type=F32)


def _ffn_kernel(*refs, ple):
    if ple:
        (x_ref, win_ref, wout_ref, g_ref, b_ref, p_ref, wpg_ref, bpg_ref, wpp_ref, g2_ref, b2_ref,
         o_ref, act_ref) = refs
    else:
        x_ref, win_ref, wout_ref, g_ref, b_ref, o_ref, act_ref = refs
    x = x_ref[...]
    xb = x.astype(BF16)
    for c in range(N_FF_CHUNKS):
        h = _dot(xb, win_ref[c])
        gate, up = h[:, :FF_CHUNK], h[:, FF_CHUNK:]
        act_ref[:, c * FF_CHUNK:(c + 1) * FF_CHUNK] = (gate * jax.nn.sigmoid(gate) * up).astype(BF16)
    y = _dot(act_ref[...], wout_ref[...])
    x1 = _layer_norm(ALPHA * x + 0.5 * y, g_ref[...], b_ref[...])
    if ple:
        gate = jax.nn.sigmoid(_dot(x1.astype(BF16), wpg_ref[...]) + bpg_ref[...])
        emb = _dot(p_ref[...].astype(BF16), wpp_ref[...])
        x1 = _layer_norm(ALPHA * x1 + gate * emb, g2_ref[...], b2_ref[...])
    o_ref[...] = x1


def _ffn_ln(x, w, layer, which, p=None, tm=512):
    n, d = x.shape
    tm = min(tm, n)
    ple = p is not None
    row = lambda w_: pl.BlockSpec((tm, w_), lambda i: (i, 0))
    in_specs = [row(d),
                _const_spec((N_FF_CHUNKS, d, 2 * FF_CHUNK), layer),
                _const_spec((D_FF, d), layer),
                _const_spec((1, d)), _const_spec((1, d))]
    ln_idx = 0 if which == 1 else 2
    args = [x, w[f"ffn{which}_w_in"], w[f"ffn{which}_w_out"], w["ln_g"][layer][ln_idx], w["ln_b"][layer][ln_idx]]
    if ple:
        in_specs += [row(PLE_DIM), _const_spec((d, d), layer), _const_spec((1, d)),
                     _const_spec((PLE_DIM, d), layer), _const_spec((1, d)), _const_spec((1, d))]
        args += [p, w["ple_w_gate"], w["ple_b_gate"][layer], w["ple_w_proj"],
                 w["ln_g"][layer][3], w["ln_b"][layer][3]]
    return pl.pallas_call(
        functools.partial(_ffn_kernel, ple=ple),
        grid=(n // tm,),
        in_specs=in_specs,
        out_specs=row(d),
        out_shape=jax.ShapeDtypeStruct((n, d), F32),
        scratch_shapes=[pltpu.VMEM((tm, D_FF), BF16)],
        compiler_params=_cparams(1),
        name=f"ffn{which}_ln",
    )(*args)


def _rope(x, tab_ref, shift):
    return (x * tab_ref[0]
            + pltpu.roll(x, LANES - shift, axis=1) * tab_ref[1]
            + pltpu.roll(x, shift, axis=1) * tab_ref[2])


def _store_with_ones(v, lo_o, hi_o):
    even_head = lax.rem(lax.broadcasted_iota(jnp.int32, v.shape, 1), LANES) < 64
    lo_o[...] = jnp.where(even_head, v, 1.0).astype(BF16)
    hi_o[...] = jnp.where(even_head, 1.0, v).astype(BF16)


def _mla_kv(ckv_b, wukv_ref, kpe_wide, km_o, vmlo_o, vmhi_o):
    kv = _dot(ckv_b, wukv_ref[...])
    for h in range(C_HEADS):
        km_o[:, h * LANES:(h + 1) * LANES] = (kv[:, h * LANES:(h + 1) * LANES] + kpe_wide).astype(BF16)
    _store_with_ones(kv[:, C_HEADS * LANES:], vmlo_o, vmhi_o)


def _mix_kernel(x_ref, w_ref, bf_ref, gq_ref, gkv_ref, wuq_ref, wukv_ref, tdiff_ref, tmla_ref, tkr_ref,
                qa_o, ka_o, kab_o, va_o, valo_o, vahi_o, logf_o, qb_o, kb_o, kbb_o, vb_o, vbb_o,
                ckv_o, kpe_o, qm_o, km_o, vmlo_o, vmhi_o):
    xb = x_ref[...].astype(BF16)

    def seg(a, b):
        return _dot(xb, w_ref[:, a:b])

    qa_o[...] = (seg(O_QA, O_KA) * LOG2E).astype(BF16)
    ka = seg(O_KA, O_VA)
    ka_o[...] = ka
    kab_o[...] = ka.astype(BF16)
    va = seg(O_VA, O_QB)
    va_o[...] = va
    _store_with_ones(va, valo_o, vahi_o)

    qb = seg(O_QB, O_KB)
    kb = seg(O_KB, O_VB)
    for g in range(4):
        sl = slice(g * LANES, (g + 1) * LANES)
        qb_o[:, sl] = (_rope(qb[:, sl], tdiff_ref, B_ROT // 2) * LOG2E).astype(BF16)
        kr_ = _rope(kb[:, sl], tdiff_ref, B_ROT // 2)
        kb_o[:, sl] = kr_
        kbb_o[:, sl] = kr_.astype(BF16)
    vb = seg(O_VB, O_DQ)
    vb_o[...] = vb
    vbb_o[...] = vb.astype(BF16)

    cq = _rms_norm(seg(O_DQ, O_DKV), gq_ref[...]).astype(BF16)
    qm = _dot(cq, wuq_ref[...]) * ((C_NOPE + C_ROPE) ** -0.5 * LOG2E)
    for h in range(C_HEADS):
        sl = slice(h * LANES, (h + 1) * LANES)
        qm_o[:, sl] = _rope(qm[:, sl], tmla_ref, C_ROPE // 2).astype(BF16)

    ckv = _rms_norm(seg(O_DKV, O_FA), gkv_ref[...])
    ckv_o[...] = ckv

    fa = seg(O_FA, O_KR)[:, :A_HEADS] + bf_ref[...]
    logf_o[...] = jnp.minimum(fa, 0.0) - jnp.log1p(jnp.exp(-jnp.abs(fa)))

    kr = _rope(seg(O_KR, MIX_PAD), tkr_ref, C_ROPE // 2)
    kpe_o[...] = kr[:, :C_ROPE]
    lane = lax.broadcasted_iota(jnp.int32, kr.shape, 1)
    _mla_kv(ckv.astype(BF16), wukv_ref, jnp.where(lane >= C_NOPE, kr, 0.0), km_o, vmlo_o, vmhi_o)


def _mix_proj(x, w, layer, tabs, tm=512):
    n, d = x.shape
    tm = min(tm, n)
    tdiff, tmla, tkr = tabs
    n_tab = tdiff.shape[1] // tm
    row = lambda w_: pl.BlockSpec((tm, w_), lambda i: (i, 0))
    tab = pl.BlockSpec((3, tm, LANES), lambda i: (0, i % n_tab, 0))
    out_widths = [(512, BF16), (512, F32), (512, BF16), (512, F32), (512, BF16), (512, BF16), (A_HEADS, F32),
                  (512, BF16), (512, F32), (512, BF16), (512, F32), (512, BF16),
                  (C_KV_LORA, F32), (C_ROPE, F32), (C_HEADS * LANES, BF16), (C_HEADS * LANES, BF16),
                  (512, BF16), (512, BF16)]
    return pl.pallas_call(
        _mix_kernel,
        grid=(n // tm,),
        in_specs=[row(d), _const_spec((d, MIX_PAD), layer), _const_spec((1, A_HEADS)),
                  _const_spec((1, C_Q_LORA)), _const_spec((1, C_KV_LORA)),
                  _const_spec((C_Q_LORA, C_HEADS * LANES), layer),
                  _const_spec((C_KV_LORA, C_HEADS * LANES + 512), layer), tab, tab, tab],
        out_specs=[row(w_) for w_, _ in out_widths],
        out_shape=[jax.ShapeDtypeStruct((n, w_), dt) for w_, dt in out_widths],
        compiler_params=_cparams(1),
        name="mix_proj",
    )(x, w["w_mix"], w["b_forget"][layer], w["mla_q_norm_g"][layer], w["mla_kv_norm_g"][layer],
      w["mla_w_uq"], w["mla_w_ukv"], tdiff, tmla, tkr)


def _mla_cache_kernel(ckv_ref, kpe_ref, wukv_ref, km_o, vmlo_o, vmhi_o):
    _mla_kv(ckv_ref[...].astype(BF16), wukv_ref, kpe_ref[...], km_o, vmlo_o, vmhi_o)


def _mla_cache_proj(ckv, kpe_wide, w, layer, tm=512):
    n = ckv.shape[0]
    row = lambda w_: pl.BlockSpec((tm, w_), lambda i: (i, 0))
    return pl.pallas_call(
        _mla_cache_kernel,
        grid=(n // tm,),
        in_specs=[row(C_KV_LORA), row(LANES), _const_spec((C_KV_LORA, C_HEADS * LANES + 512), layer)],
        out_specs=[row(C_HEADS * LANES), row(512), row(512)],
        out_shape=[jax.ShapeDtypeStruct((n, C_HEADS * LANES), BF16), jax.ShapeDtypeStruct((n, 512), BF16),
                   jax.ShapeDtypeStruct((n, 512), BF16)],
        compiler_params=_cparams(1),
        name="mla_cache_proj",
    )(ckv, kpe_wide, w["mla_w_ukv"])


def _cumsum_kernel(x_ref, o_ref):
    x = x_ref[0]
    lane = lax.broadcasted_iota(jnp.int32, x.shape, 1)
    s = 1
    while s < x.shape[1]:
        x = x + jnp.where(lane >= s, pltpu.roll(x, s, axis=1), 0.0)
        s *= 2
    o_ref[0] = x * LOG2E


def _cumsum_time(x):
    b, h, t = x.shape
    spec = pl.BlockSpec((1, h, t), lambda i: (i, 0, 0))
    return pl.pallas_call(_cumsum_kernel, grid=(b,), in_specs=[spec], out_specs=spec,
                          out_shape=jax.ShapeDtypeStruct(x.shape, F32), compiler_params=_cparams(1),
                          name="cumsum_time")(x)


def _attn_kernel(*refs, mode, tq, tk, q_off, lam_init):
    if mode == "fox":
        q_ref, k_ref, vlo_ref, vhi_ref, fq_ref, fk_ref, o_ref, m_s, acc_s = refs
    elif mode == "diff":
        q_ref, k_ref, v_ref, lam_ref, g_ref, o_ref, m_s, acc_s = refs
    else:
        q_ref, k_ref, vlo_ref, vhi_ref, o_ref, m_s, acc_s = refs
    qi = pl.program_id(1)
    lo = lax.broadcasted_iota(jnp.int32, (tq, LANES), 1) < 64

    qs = []
    for g in range(4):
        if mode == "mla":
            qs.append([q_ref[0, :, (2 * g + hh) * LANES:(2 * g + hh + 1) * LANES] for hh in range(2)])
        else:
            qf = q_ref[0, :, g * LANES:(g + 1) * LANES].astype(F32)
            qs.append(jnp.concatenate([jnp.where(lo, qf, 0.0), jnp.where(lo, 0.0, qf)], axis=0).astype(BF16))
    if mode == "fox":
        fqs = [jnp.concatenate([jnp.broadcast_to(fq_ref[0, :, 2 * g + hh:2 * g + hh + 1], (tq, LANES))
                                for hh in range(2)], axis=0) for g in range(4)]

    m_s[...] = jnp.full(m_s.shape, NEG, F32)
    acc_s[...] = jnp.zeros(acc_s.shape, F32)

    q0 = q_off + qi * tq
    n_full = q0 // tk
    dims = (((1,), (1,)), ((), ()))

    def scores(j):
        out = []
        ks = pl.multiple_of(j * tk, tk)
        for g in range(4):
            if mode == "mla":
                s = jnp.concatenate(
                    [lax.dot_general(qs[g][hh], k_ref[0, pl.ds(ks, tk), (2 * g + hh) * LANES:(2 * g + hh + 1) * LANES],
                                     dims, preferred_element_type=F32) for hh in range(2)], axis=0)
            else:
                s = lax.dot_general(qs[g], k_ref[0, pl.ds(ks, tk), g * LANES:(g + 1) * LANES], dims,
                                    preferred_element_type=F32)
            if mode == "fox":
                s = jnp.concatenate([s[:tq] - fk_ref[0, j, 2 * g:2 * g + 1, :],
                                     s[tq:] - fk_ref[0, j, 2 * g + 1:2 * g + 2, :]], axis=0)
            out.append(s)
        return out

    def softmax_pv(j, s_blk, masked):
        ks = pl.multiple_of(j * tk, tk)
        if masked:
            row = lax.broadcasted_iota(jnp.int32, (2 * tq, tk), 0)
            qpos = q0 + jnp.where(row >= tq, row - tq, row)
            kpos = ks + lax.broadcasted_iota(jnp.int32, (2 * tq, tk), 1)
            if mode == "fox":
                vis = kpos <= qpos
            else:
                shift = CHUNK.bit_length() - 1
                vis = jnp.right_shift(kpos, shift) <= jnp.right_shift(qpos, shift)
        for g in range(4):
            s = s_blk[g]
            if masked:
                s = jnp.where(vis, s, NEG)
            m_old = m_s[g]
            mx = jnp.max(s, axis=-1, keepdims=True)
            if mode == "fox":
                m_new = jnp.maximum(m_old, mx + fqs[g])
                cen = m_new - fqs[g]
            else:
                m_new = jnp.maximum(m_old, mx)
                cen = m_new
            alpha = jnp.exp2(m_old - m_new)
            p = jnp.exp2(s - jnp.concatenate([cen] * (tk // LANES), axis=1)).astype(BF16)
            cols = slice(g * LANES, (g + 1) * LANES)
            if mode == "diff":
                vaug = jnp.concatenate([v_ref[0, pl.ds(ks, tk), cols], jnp.ones((tk, LANES), BF16)], axis=1)
                acc_s[g] = jnp.concatenate([alpha, alpha], axis=1) * acc_s[g] + _dot(p, vaug)
            else:
                pv = jnp.concatenate([_dot(p[:tq], vlo_ref[0, pl.ds(ks, tk), cols]),
                                      _dot(p[tq:], vhi_ref[0, pl.ds(ks, tk), cols])], axis=0)
                acc_s[g] = alpha * acc_s[g] + pv
            m_s[g] = m_new

    def body(j, carry):
        softmax_pv(j, scores(j), False)
        return carry

    lax.fori_loop(0, n_full, body, 0)
    softmax_pv(n_full, scores(n_full), True)

    if mode == "diff":
        lam = (jnp.exp(jnp.sum(lam_ref[0:1, :] * lam_ref[1:2, :], axis=-1, keepdims=True))
               - jnp.exp(jnp.sum(lam_ref[2:3, :] * lam_ref[3:4, :], axis=-1, keepdims=True)) + lam_init)
    for g in range(4):
        acc = acc_s[g]
        if mode == "diff":
            o = acc[:, :LANES] / acc[:, LANES:]
            out = _rms_norm(o[:tq] - lam * o[tq:], g_ref[...]) * (1.0 - lam_init)
        else:
            a_lo, a_hi = acc[:tq], acc[tq:]
            out = jnp.where(lo, a_lo / pltpu.roll(a_lo, 64, axis=1), a_hi / pltpu.roll(a_hi, 64, axis=1))
        o_ref[0, :, g * LANES:(g + 1) * LANES] = out.astype(BF16)


def _attention(mode, q, k, vs, extra, *, tq, tk, q_off, lam_init=0.0):
    b, t_q, qw = q.shape
    t_k = k.shape[1]
    assert t_q % tq == 0 and t_k % tk == 0
    assert all((q_off + i * tq) // tk == (q_off + (i + 1) * tq - 1) // tk for i in range(t_q // tq))
    in_specs = [pl.BlockSpec((1, tq, qw), lambda bi, qi: (bi, qi, 0)),
                pl.BlockSpec((1, t_k, qw), lambda bi, qi: (bi, 0, 0))]
    in_specs += [pl.BlockSpec((1, t_k, 4 * LANES), lambda bi, qi: (bi, 0, 0))] * len(vs)
    if mode == "fox":
        in_specs += [pl.BlockSpec((1, tq, A_HEADS), lambda bi, qi: (bi, qi, 0)),
                     pl.BlockSpec((1, t_k // tk, A_HEADS, tk), lambda bi, qi: (bi, 0, 0, 0))]
    elif mode == "diff":
        in_specs += [_const_spec((4, B_HD)), _const_spec((1, 2 * B_HD))]
    acc_w = 2 * LANES if mode == "diff" else LANES
    return pl.pallas_call(
        functools.partial(_attn_kernel, mode=mode, tq=tq, tk=tk, q_off=q_off, lam_init=lam_init),
        grid=(b, t_q // tq),
        in_specs=in_specs,
        out_specs=pl.BlockSpec((1, tq, 4 * LANES), lambda bi, qi: (bi, qi, 0)),
        out_shape=jax.ShapeDtypeStruct((b, t_q, 4 * LANES), BF16),
        scratch_shapes=[pltpu.VMEM((4, 2 * tq, LANES), F32), pltpu.VMEM((4, 2 * tq, acc_w), F32)],
        compiler_params=_cparams(2),
        name=f"attn_{mode}",
    )(q, k, *vs, *extra)


def _merge_kernel(h_ref, oa_ref, ob_ref, oc_ref, wg_ref, bg_ref, wb_ref, wo_ref, g_ref, b_ref, o_ref):
    h = h_ref[...]
    hb = h.astype(BF16)
    d = h.shape[1]
    merged = None
    for br, o_br in enumerate((oa_ref, ob_ref, oc_ref)):
        gate = jax.nn.sigmoid(_dot(hb, wg_ref[:, br * d:(br + 1) * d]) + bg_ref[:, br * d:(br + 1) * d])
        term = gate * _dot(o_br[...], wb_ref[br])
        merged = term if merged is None else merged + term
    mix = _dot(merged.astype(BF16), wo_ref[...])
    o_ref[...] = _layer_norm(ALPHA * h + mix, g_ref[...], b_ref[...])


def _merge(h, oa, ob, oc, w, layer, tm=512):
    n, d = h.shape
    tm = min(tm, n)
    row = lambda w_: pl.BlockSpec((tm, w_), lambda i: (i, 0))
    return pl.pallas_call(
        _merge_kernel,
        grid=(n // tm,),
        in_specs=[row(d), row(BRANCH_W), row(BRANCH_W), row(BRANCH_W),
                  _const_spec((d, 3 * d), layer), _const_spec((1, 3 * d)),
                  _const_spec((3, BRANCH_W, d), layer), _const_spec((d, d), layer),
                  _const_spec((1, d)), _const_spec((1, d))],
        out_specs=row(d),
        out_shape=jax.ShapeDtypeStruct((n, d), F32),
        compiler_params=_cparams(1),
        name="merge",
    )(h, oa, ob, oc, w["w_gate"], w["b_gate"][layer], w["w_branch"], w["w_out"],
      w["ln_g"][layer][1], w["ln_b"][layer][1])


def _rope_table(pos, groups, rows):
    t = pos.shape[0]
    one, zero = jnp.ones((t, 1), F32), jnp.zeros((t, 1), F32)
    c_cols, s1_cols, s2_cols = [], [], []
    lane = 0
    for start, half, theta in groups:
        inv = jnp.exp(-math.log(theta) * jnp.arange(half, dtype=F32) * (2.0 / (2 * half)))
        ang = pos.astype(F32)[:, None] * inv[None, :]
        cos, sin = jnp.cos(ang), jnp.sin(ang)
        gap = start - lane
        c_cols += [jnp.tile(one, (1, gap)), cos, cos]
        s1_cols += [jnp.tile(zero, (1, gap)), -sin, jnp.zeros_like(sin)]
        s2_cols += [jnp.tile(zero, (1, gap)), jnp.zeros_like(sin), sin]
        lane = start + 2 * half
    gap = LANES - lane
    tabs = [jnp.concatenate(cols + [jnp.tile(fill, (1, gap))], axis=1)
            for cols, fill in ((c_cols, one), (s1_cols, zero), (s2_cols, zero))]
    tab = jnp.stack(tabs, axis=0)
    return jnp.tile(tab, (1, rows // t, 1)) if rows > t else tab


def _rope_tables(pos, rows):
    h_b, h_c = B_ROT // 2, C_ROPE // 2
    return (_rope_table(pos, [(0, h_b, ROPE_THETA), (B_HD, h_b, ROPE_THETA)], rows),
            _rope_table(pos, [(C_NOPE, h_c, MLA_ROPE_THETA)], rows),
            _rope_table(pos, [(0, h_c, MLA_ROPE_THETA), (C_NOPE, h_c, MLA_ROPE_THETA)], rows))


def _prepare_weights(ffn1_w_in, ffn1_w_out, ffn2_w_in, ffn2_w_out, ln_g, ln_b, w_in_mix, b_forget,
                     diff_lambda, diff_norm_g, mla_q_norm_g, mla_w_uq, mla_kv_norm_g, mla_w_ukv,
                     w_branch, w_gate, b_gate, w_out, ple_w_gate, ple_b_gate, ple_w_proj):
    depth, d = w_in_mix.shape[0], w_in_mix.shape[1]

    def ffn_in(w_in):
        gate = w_in[..., :D_FF].reshape(depth, d, N_FF_CHUNKS, FF_CHUNK)
        up = w_in[..., D_FF:].reshape(depth, d, N_FF_CHUNKS, FF_CHUNK)
        return jnp.concatenate([gate, up], axis=-1).transpose(0, 2, 1, 3).astype(BF16)

    offs = [sum(MIX_SPLITS[:i]) for i in range(1, len(MIX_SPLITS))]
    qa, ka, va, fa, qb, kb, vb, dq, dkv, kr = jnp.split(w_in_mix, offs, axis=-1)
    z = lambda n_: jnp.zeros((depth, d, n_), F32)
    w_mix = jnp.concatenate([qa * (A_HD ** -0.5), ka, va, qb * (B_HD ** -0.5), kb, vb, dq, dkv,
                             fa, z(LANES - A_HEADS), kr, z(C_NOPE - C_ROPE), kr, z(LANES - C_NOPE - C_ROPE)],
                            axis=-1).astype(BF16)
    w_uq = mla_w_uq.reshape(depth, C_Q_LORA, C_HEADS, C_NOPE + C_ROPE)
    w_uq = jnp.pad(w_uq, ((0, 0), (0, 0), (0, 0), (0, LANES - C_NOPE - C_ROPE)))
    w_uq = w_uq.reshape(depth, C_Q_LORA, C_HEADS * LANES).astype(BF16)
    w_uk = jnp.pad(mla_w_ukv[..., :C_NOPE], ((0, 0), (0, 0), (0, 0), (0, LANES - C_NOPE)))
    w_uv = mla_w_ukv[..., C_NOPE:]
    w_ukv = jnp.concatenate([w_uk.reshape(depth, C_KV_LORA, C_HEADS * LANES),
                             w_uv.reshape(depth, C_KV_LORA, C_HEADS * C_VD)], axis=-1).astype(BF16)
    vec = lambda a: a[:, None, :]
    return dict(
        ffn1_w_in=ffn_in(ffn1_w_in), ffn1_w_out=ffn1_w_out.astype(BF16),
        ffn2_w_in=ffn_in(ffn2_w_in), ffn2_w_out=ffn2_w_out.astype(BF16),
        ln_g=ln_g[:, :, None, :], ln_b=ln_b[:, :, None, :],
        w_mix=w_mix, b_forget=vec(b_forget), diff_lambda=diff_lambda, diff_norm_g=vec(diff_norm_g),
        mla_q_norm_g=vec(mla_q_norm_g), mla_kv_norm_g=vec(mla_kv_norm_g), mla_w_uq=w_uq, mla_w_ukv=w_ukv,
        w_branch=w_branch.astype(BF16), w_gate=w_gate.astype(BF16), b_gate=vec(b_gate),
        w_out=w_out.astype(BF16), ple_w_gate=ple_w_gate.astype(BF16), ple_b_gate=vec(ple_b_gate),
        ple_w_proj=ple_w_proj.astype(BF16))


def _layer_step(x, p, layer, w, tabs, cache, past):
    b, t, d = x.shape
    n = b * t
    lam_init = 0.8 - 0.6 * math.exp(-0.3 * layer)
    x1 = _ffn_ln(x.reshape(n, d), w, layer, 1)
    (qa, ka, kab, va, valo, vahi, logf, qb, kb, kbb, vb, vbb, ckv, kpe, qm, km, vmlo, vmhi) = _mix_proj(
        x1, w, layer, tabs)
    r3 = lambda a: a.reshape(b, t, a.shape[-1])
    new_state = (r3(ka).reshape(b, t, A_HEADS, A_HD), r3(va).reshape(b, t, A_HEADS, A_HD), r3(logf),
                 r3(kb).reshape(b, t, B_HEADS, 2, B_HD), r3(vb).reshape(b, t, B_HEADS, 2 * B_HD), r3(ckv), r3(kpe))
    qa, kab, valo, vahi, qb, kbb, vbb, qm, km, vmlo, vmhi = map(
        r3, (qa, kab, valo, vahi, qb, kbb, vbb, qm, km, vmlo, vmhi))
    logf_t = r3(logf).transpose(0, 2, 1)
    if cache is None:
        tq = tk = min(256, t)
        q_off = 0
        cum = _cumsum_time(logf_t)
        fq_src = cum
    else:
        c_ka, c_va, c_logf, c_kb, c_vb, c_ckv, c_kpe = cache
        t_k = -(-(past + t) // LANES) * LANES
        pad = t_k - past - t
        tq, tk, q_off = t, t_k, past

        def with_cache(c, new):
            c = c.reshape(b, past, -1).astype(BF16)
            return jnp.concatenate([c, new, jnp.zeros((b, pad, new.shape[-1]), BF16)], axis=1)

        c_va = c_va.reshape(b, past, -1)
        even_head = (jnp.arange(c_va.shape[-1]) % LANES) < 64
        kab, kbb, vbb = (with_cache(c, a) for c, a in ((c_ka, kab), (c_kb, kbb), (c_vb, vbb)))
        valo = with_cache(jnp.where(even_head, c_va, 1.0), valo)
        vahi = with_cache(jnp.where(even_head, 1.0, c_va), vahi)
        kpe_wide = jnp.pad(c_kpe.reshape(b * past, C_ROPE), ((0, 0), (C_NOPE, LANES - C_NOPE - C_ROPE)))
        km_c, vmlo_c, vmhi_c = _mla_cache_proj(c_ckv.reshape(b * past, C_KV_LORA), kpe_wide, w, layer)
        km, vmlo, vmhi = (with_cache(c, a) for c, a in ((km_c, km), (vmlo_c, vmlo), (vmhi_c, vmhi)))
        logf_all = jnp.concatenate([c_logf.astype(F32).transpose(0, 2, 1), logf_t,
                                    jnp.zeros((b, A_HEADS, pad), F32)], axis=2)
        cum = _cumsum_time(logf_all)
        fq_src = cum[:, :, past:past + t]
    fq = fq_src.transpose(0, 2, 1)
    fk = cum.reshape(b, A_HEADS, cum.shape[2] // tk, tk).transpose(0, 2, 1, 3)
    oa = _attention("fox", qa, kab, (valo, vahi), (fq, fk), tq=tq, tk=tk, q_off=q_off)
    ob = _attention("diff", qb, kbb, (vbb,), (w["diff_lambda"][layer], w["diff_norm_g"][layer]),
                    tq=tq, tk=tk, q_off=q_off, lam_init=lam_init)
    oc = _attention("mla", qm, km, (vmlo, vmhi), (), tq=tq, tk=tk, q_off=q_off)
    f2 = lambda a: a.reshape(n, a.shape[-1])
    x2 = _merge(x1, f2(oa), f2(ob), f2(oc), w, layer)
    x3 = _ffn_ln(x2, w, layer, 2, p=p.reshape(n, PLE_DIM))
    return x3.reshape(b, t, d), new_state


def kernel(x_prompt, x_sample, p_prompt, p_sample, cache_fox_k, cache_fox_v, cache_fox_logf, cache_diff_k, cache_diff_v, cache_mla_ckv, cache_mla_kpe, ffn1_w_in, ffn1_w_out, ffn2_w_in, ffn2_w_out, ln_g, ln_b, w_in_mix, b_forget, diff_lambda, diff_norm_g, mla_q_norm_g, mla_w_uq, mla_kv_norm_g, mla_w_ukv, w_branch, w_gate, b_gate, w_out, ple_w_gate, ple_b_gate, ple_w_proj):
    w = _prepare_weights(ffn1_w_in, ffn1_w_out, ffn2_w_in, ffn2_w_out, ln_g, ln_b, w_in_mix, b_forget,
                         diff_lambda, diff_norm_g, mla_q_norm_g, mla_w_uq, mla_kv_norm_g, mla_w_ukv,
                         w_branch, w_gate, b_gate, w_out, ple_w_gate, ple_b_gate, ple_w_proj)
    depth = w_in_mix.shape[0]
    t_p, t_s = x_prompt.shape[1], x_sample.shape[1]
    past = cache_fox_k.shape[2]
    n_s = x_sample.shape[0] * t_s
    tabs_p = _rope_tables(jnp.arange(t_p, dtype=jnp.int32), t_p)
    tabs_s = _rope_tables(past + jnp.arange(t_s, dtype=jnp.int32), min(512, n_s))

    xp, xs = x_prompt, x_sample
    states_p, states_s = [], []
    for i in range(depth):
        xp, st_p = _layer_step(xp, p_prompt[i], i, w, tabs_p, None, 0)
        states_p.append(st_p)
        cache_i = (cache_fox_k[i], cache_fox_v[i], cache_fox_logf[i], cache_diff_k[i], cache_diff_v[i],
                   cache_mla_ckv[i], cache_mla_kpe[i])
        xs, st_s = _layer_step(xs, p_sample[i], i, w, tabs_s, cache_i, past)
        states_s.append(st_s)
    sp = [jnp.stack([st[j] for st in states_p], axis=0) for j in range(7)]
    ss = [jnp.stack([st[j] for st in states_s], axis=0) for j in range(7)]
    return (xp, xs, sp[0], ss[0], sp[1], ss[1], sp[2], ss[2], sp[3], ss[3], sp[4], ss[4],
            sp[5], ss[5], sp[6], ss[6])
```

```python
import functools
import math

import jax
import jax.numpy as jnp
from jax import lax
from jax.experimental import pallas as pl
from jax.experimental.pallas import tpu as pltpu

F32 = jnp.float32
BF16 = jnp.bfloat16

D_MODEL = 1024
DEPTH = 2
CHUNK = 64
PLE_DIM = 256
D_FF = 2816
LN_EPS = 1e-5
RMS_EPS = 1e-6
ALPHA = (2 * DEPTH) ** 0.25
A_HEADS, A_HD = 8, 64
B_HEADS, B_HD = 4, 64
B_ROT = B_HD // 4
ROPE_THETA = 500000.0
C_HEADS, C_NOPE, C_ROPE, C_VD = 8, 64, 32, 64
C_Q_LORA, C_KV_LORA = 384, 256
MLA_ROPE_THETA = 10000.0
BRANCH_W = 512
MIX_SPLITS = (512, 512, 512, 8, 512, 512, 512, C_Q_LORA, C_KV_LORA, C_ROPE)

LANES = 128
FF_CHUNK = 256
N_FF_CHUNKS = D_FF // FF_CHUNK
O_QA, O_KA, O_VA, O_QB, O_KB, O_VB, O_DQ, O_DKV, O_FA, O_KR, MIX_PAD = (
    0, 512, 1024, 1536, 2048, 2560, 3072, 3456, 3712, 3840, 3968)
NEG = -1e30
LOG2E = math.log2(math.e)
VMEM_LIMIT = 56 * 1024 * 1024


def _cparams(n_axes):
    return pltpu.CompilerParams(dimension_semantics=("parallel",) * n_axes, vmem_limit_bytes=VMEM_LIMIT)


def _const_spec(shape, layer=None):
    if layer is None:
        return pl.BlockSpec(shape, lambda *_: (0,) * len(shape), pipeline_mode=pl.Buffered(1))
    return pl.BlockSpec((None,) + tuple(shape), lambda *_: (layer,) + (0,) * len(shape),
                        pipeline_mode=pl.Buffered(1))


def _layer_norm(z, g, b):
    mu = jnp.mean(z, axis=-1, keepdims=True)
    zc = z - mu
    var = jnp.mean(zc * zc, axis=-1, keepdims=True)
    return zc * lax.rsqrt(var + LN_EPS) * g + b


def _rms_norm(x, g):
    return x * lax.rsqrt(jnp.mean(x * x, axis=-1, keepdims=True) + RMS_EPS) * g


def _dot(a, b):
    return jnp.dot(a, b, preferred_element_type=F32)


def _ffn_kernel(*refs, ple):
    if ple:
        (x_ref, win_ref, wout_ref, g_ref, b_ref, p_ref, wpg_ref, bpg_ref, wpp_ref, g2_ref, b2_ref,
         o_ref, act_ref) = refs
    else:
        x_ref, win_ref, wout_ref, g_ref, b_ref, o_ref, act_ref = refs
    x = x_ref[...]
    xb = x.astype(BF16)
    for c in range(N_FF_CHUNKS):
        cols = slice(c * FF_CHUNK, (c + 1) * FF_CHUNK)
        gate = _dot(xb, win_ref[:, cols])
        up = _dot(xb, win_ref[:, D_FF + c * FF_CHUNK:D_FF + (c + 1) * FF_CHUNK])
        act_ref[:, cols] = (gate * jax.nn.sigmoid(gate) * up).astype(BF16)
    y = _dot(act_ref[...], wout_ref[...])
    x1 = _layer_norm(ALPHA * x + 0.5 * y, g_ref[...], b_ref[...])
    if ple:
        gate = jax.nn.sigmoid(_dot(x1.astype(BF16), wpg_ref[...]) + bpg_ref[...])
        emb = _dot(p_ref[...].astype(BF16), wpp_ref[...])
        x1 = _layer_norm(ALPHA * x1 + gate * emb, g2_ref[...], b2_ref[...])
    o_ref[...] = x1


def _ffn_ln(x, w, layer, which, p=None, tm=512):
    n, d = x.shape
    tm = min(tm, n)
    ple = p is not None
    row = lambda w_: pl.BlockSpec((tm, w_), lambda i: (i, 0))
    in_specs = [row(d),
                _const_spec((d, 2 * D_FF), layer),
                _const_spec((D_FF, d), layer),
                _const_spec((1, d)), _const_spec((1, d))]
    ln_idx = 0 if which == 1 else 2
    args = [x, w[f"ffn{which}_w_in"], w[f"ffn{which}_w_out"], w["ln_g"][layer][ln_idx], w["ln_b"][layer][ln_idx]]
    if ple:
        in_specs += [row(PLE_DIM), _const_spec((d, d), layer), _const_spec((1, d)),
                     _const_spec((PLE_DIM, d), layer), _const_spec((1, d)), _const_spec((1, d))]
        args += [p, w["ple_w_gate"], w["ple_b_gate"][layer], w["ple_w_proj"],
                 w["ln_g"][layer][3], w["ln_b"][layer][3]]
    return pl.pallas_call(
        functools.partial(_ffn_kernel, ple=ple),
        grid=(n // tm,),
        in_specs=in_specs,
        out_specs=row(d),
        out_shape=jax.ShapeDtypeStruct((n, d), F32),
        scratch_shapes=[pltpu.VMEM((tm, D_FF), BF16)],
        compiler_params=_cparams(1),
        name=f"ffn{which}_ln",
    )(*args)


def _rope(x, tab_ref, shift):
    return (x * tab_ref[0]
            + pltpu.roll(x, LANES - shift, axis=1) * tab_ref[1]
            + pltpu.roll(x, shift, axis=1) * tab_ref[2])


def _store_with_ones(v, lo_o, hi_o):
    even_head = lax.rem(lax.broadcasted_iota(jnp.int32, v.shape, 1), LANES) < 64
    lo_o[...] = jnp.where(even_head, v, 1.0).astype(BF16)
    hi_o[...] = jnp.where(even_head, 1.0, v).astype(BF16)


def _mla_kv(ckv_b, wukv_ref, kpe_wide, km_o, vmlo_o, vmhi_o):
    kv = _dot(ckv_b, wukv_ref[...])
    for h in range(C_HEADS):
        km_o[:, h * LANES:(h + 1) * LANES] = (kv[:, h * LANES:(h + 1) * LANES] + kpe_wide).astype(BF16)
    _store_with_ones(kv[:, C_HEADS * LANES:], vmlo_o, vmhi_o)


N_MIX_IN = 10
STATE_OUTS = (1, 3, 6, 8, 10, 12, 13)


def _mix_kernel(*refs):
    x_ref, w_ref, bf_ref, gq_ref, gkv_ref, wuq_ref, wukv_ref, tdiff_ref, tmla_ref, tkr_ref = refs[:N_MIX_IN]
    (qa_o, ka_o, kab_o, va_o, valo_o, vahi_o, logf_o, qb_o, kb_o, kbb_o, vb_o, vbb_o,
     ckv_o, kpe_o, qm_o, km_o, vmlo_o, vmhi_o) = refs[-18:]
    xb = x_ref[...].astype(BF16)

    def seg(a, b):
        return _dot(xb, w_ref[:, a:b])

    qa_o[...] = (seg(O_QA, O_KA) * LOG2E).astype(BF16)
    ka = seg(O_KA, O_VA)
    ka_o[...] = ka
    kab_o[...] = ka.astype(BF16)
    va = seg(O_VA, O_QB)
    va_o[...] = va
    _store_with_ones(va, valo_o, vahi_o)

    qb = seg(O_QB, O_KB)
    kb = seg(O_KB, O_VB)
    for g in range(4):
        sl = slice(g * LANES, (g + 1) * LANES)
        qb_o[:, sl] = (_rope(qb[:, sl], tdiff_ref, B_ROT // 2) * LOG2E).astype(BF16)
        kr_ = _rope(kb[:, sl], tdiff_ref, B_ROT // 2)
        kb_o[:, sl] = kr_
        kbb_o[:, sl] = kr_.astype(BF16)
    vb = seg(O_VB, O_DQ)
    vb_o[...] = vb
    vbb_o[...] = vb.astype(BF16)

    cq = _rms_norm(seg(O_DQ, O_DKV), gq_ref[...]).astype(BF16)
    qm = _dot(cq, wuq_ref[...]) * ((C_NOPE + C_ROPE) ** -0.5 * LOG2E)
    for h in range(C_HEADS):
        sl = slice(h * LANES, (h + 1) * LANES)
        qm_o[:, sl] = _rope(qm[:, sl], tmla_ref, C_ROPE // 2).astype(BF16)

    ckv = _rms_norm(seg(O_DKV, O_FA), gkv_ref[...])
    ckv_o[...] = ckv

    fa = seg(O_FA, O_KR)[:, :A_HEADS] + bf_ref[...]
    logf_o[...] = jnp.minimum(fa, 0.0) - jnp.log1p(jnp.exp(-jnp.abs(fa)))

    kr = _rope(seg(O_KR, MIX_PAD), tkr_ref, C_ROPE // 2)
    kpe_o[...] = kr[:, :C_ROPE]
    lane = lax.broadcasted_iota(jnp.int32, kr.shape, 1)
    _mla_kv(ckv.astype(BF16), wukv_ref, jnp.where(lane >= C_NOPE, kr, 0.0), km_o, vmlo_o, vmhi_o)


def _mix_proj(x, w, layer, tabs, states, tm=512):
    n, d = x.shape
    tm = min(tm, n)
    depth = w["w_mix"].shape[0]
    tdiff, tmla, tkr = tabs
    assert n % tm == 0 and tdiff.shape[1] % tm == 0
    n_tab = tdiff.shape[1] // tm
    row = lambda w_: pl.BlockSpec((tm, w_), lambda i: (i, 0))
    slab = lambda w_: pl.BlockSpec((None, tm, w_), lambda i: (layer, i, 0))
    tab = pl.BlockSpec((3, tm, LANES), lambda i: (0, i % n_tab, 0))
    out_widths = [(512, BF16), (512, F32), (512, BF16), (512, F32), (512, BF16), (512, BF16), (A_HEADS, F32),
                  (512, BF16), (512, F32), (512, BF16), (512, F32), (512, BF16),
                  (C_KV_LORA, F32), (C_ROPE, F32), (C_HEADS * LANES, BF16), (C_HEADS * LANES, BF16),
                  (512, BF16), (512, BF16)]
    in_specs = [row(d), _const_spec((d, MIX_PAD), layer), _const_spec((1, A_HEADS)),
                _const_spec((1, C_Q_LORA)), _const_spec((1, C_KV_LORA)),
                _const_spec((C_Q_LORA, C_HEADS * LANES), layer),
                _const_spec((C_KV_LORA, C_HEADS * LANES + 512), layer), tab, tab, tab]
    args = [x, w["w_mix"], w["b_forget"][layer], w["mla_q_norm_g"][layer], w["mla_kv_norm_g"][layer],
            w["mla_w_uq"], w["mla_w_ukv"], tdiff, tmla, tkr]
    assert len(in_specs) == N_MIX_IN
    aliases = {}
    if states is not None:
        in_specs += [pl.BlockSpec(memory_space=pl.ANY)] * len(STATE_OUTS)
        args += list(states)
        aliases = {N_MIX_IN + j: o for j, o in enumerate(STATE_OUTS)}
    return pl.pallas_call(
        _mix_kernel,
        grid=(n // tm,),
        in_specs=in_specs,
        out_specs=[slab(w_) if i in STATE_OUTS else row(w_) for i, (w_, _) in enumerate(out_widths)],
        out_shape=[jax.ShapeDtypeStruct((depth, n, w_) if i in STATE_OUTS else (n, w_), dt)
                   for i, (w_, dt) in enumerate(out_widths)],
        input_output_aliases=aliases,
        compiler_params=_cparams(1),
        name="mix_proj",
    )(*args)


def _mla_cache_kernel(ckv_ref, kpe_ref, wukv_ref, km_o, vmlo_o, vmhi_o):
    _mla_kv(ckv_ref[...].astype(BF16), wukv_ref, kpe_ref[...], km_o, vmlo_o, vmhi_o)


def _mla_cache_proj(ckv, kpe_wide, w, layer, tm=512):
    n = ckv.shape[0]
    row = lambda w_: pl.BlockSpec((tm, w_), lambda i: (i, 0))
    return pl.pallas_call(
        _mla_cache_kernel,
        grid=(n // tm,),
        in_specs=[row(C_KV_LORA), row(LANES), _const_spec((C_KV_LORA, C_HEADS * LANES + 512), layer)],
        out_specs=[row(C_HEADS * LANES), row(512), row(512)],
        out_shape=[jax.ShapeDtypeStruct((n, C_HEADS * LANES), BF16), jax.ShapeDtypeStruct((n, 512), BF16),
                   jax.ShapeDtypeStruct((n, 512), BF16)],
        compiler_params=_cparams(1),
        name="mla_cache_proj",
    )(ckv, kpe_wide, w["mla_w_ukv"])


def _cumsum_kernel(x_ref, o_ref):
    x = x_ref[0]
    lane = lax.broadcasted_iota(jnp.int32, x.shape, 1)
    s = 1
    while s < x.shape[1]:
        x = x + jnp.where(lane >= s, pltpu.roll(x, s, axis=1), 0.0)
        s *= 2
    o_ref[0] = x * LOG2E


def _cumsum_time(x):
    b, h, t = x.shape
    spec = pl.BlockSpec((1, h, t), lambda i: (i, 0, 0))
    return pl.pallas_call(_cumsum_kernel, grid=(b,), in_specs=[spec], out_specs=spec,
                          out_shape=jax.ShapeDtypeStruct(x.shape, F32), compiler_params=_cparams(1),
                          name="cumsum_time")(x)


def _attn_kernel(*refs, mode, tq, tk, q_off, lam_init):
    if mode == "fox":
        q_ref, k_ref, vlo_ref, vhi_ref, fq_ref, fk_ref, o_ref, m_s, acc_s = refs
    elif mode == "diff":
        q_ref, k_ref, v_ref, lam_ref, g_ref, o_ref, m_s, acc_s = refs
    else:
        q_ref, k_ref, vlo_ref, vhi_ref, o_ref, m_s, acc_s = refs
    qi = pl.program_id(1)
    lo = lax.broadcasted_iota(jnp.int32, (tq, LANES), 1) < 64

    qs = []
    for g in range(4):
        if mode == "mla":
            qs.append([q_ref[0, :, (2 * g + hh) * LANES:(2 * g + hh + 1) * LANES] for hh in range(2)])
        else:
            qf = q_ref[0, :, g * LANES:(g + 1) * LANES].astype(F32)
            qs.append(jnp.concatenate([jnp.where(lo, qf, 0.0), jnp.where(lo, 0.0, qf)], axis=0).astype(BF16))
    if mode == "fox":
        fqs = [jnp.concatenate([jnp.broadcast_to(fq_ref[0, :, 2 * g + hh:2 * g + hh + 1], (tq, LANES))
                                for hh in range(2)], axis=0) for g in range(4)]

    m_s[...] = jnp.full(m_s.shape, NEG, F32)
    acc_s[...] = jnp.zeros(acc_s.shape, F32)

    q0 = q_off + qi * tq
    n_full = q0 // tk
    dims = (((1,), (1,)), ((), ()))

    def scores(j):
        out = []
        ks = pl.multiple_of(j * tk, tk)
        for g in range(4):
            if mode == "mla":
                s = jnp.concatenate(
                    [lax.dot_general(qs[g][hh], k_ref[0, pl.ds(ks, tk), (2 * g + hh) * LANES:(2 * g + hh + 1) * LANES],
                                     dims, preferred_element_type=F32) for hh in range(2)], axis=0)
            else:
                s = lax.dot_general(qs[g], k_ref[0, pl.ds(ks, tk), g * LANES:(g + 1) * LANES], dims,
                                    preferred_element_type=F32)
            if mode == "fox":
                s = jnp.concatenate([s[:tq] - fk_ref[0, j, 2 * g:2 * g + 1, :],
                                     s[tq:] - fk_ref[0, j, 2 * g + 1:2 * g + 2, :]], axis=0)
            out.append(s)
        return out

    def softmax_pv(j, s_blk, masked):
        ks = pl.multiple_of(j * tk, tk)
        if masked:
            row = lax.broadcasted_iota(jnp.int32, (2 * tq, tk), 0)
            qpos = q0 + jnp.where(row >= tq, row - tq, row)
            kpos = ks + lax.broadcasted_iota(jnp.int32, (2 * tq, tk), 1)
            if mode == "fox":
                vis = kpos <= qpos
            else:
                shift = CHUNK.bit_length() - 1
                vis = jnp.right_shift(kpos, shift) <= jnp.right_shift(qpos, shift)
        for g in range(4):
            s = s_blk[g]
            if masked:
                s = jnp.where(vis, s, NEG)
            m_old = m_s[g]
            mx = jnp.max(s, axis=-1, keepdims=True)
            if mode == "fox":
                m_new = jnp.maximum(m_old, mx + fqs[g])
                cen = m_new - fqs[g]
            else:
                m_new = jnp.maximum(m_old, mx)
                cen = m_new
            alpha = jnp.exp2(m_old - m_new)
            p = jnp.exp2(s - jnp.concatenate([cen] * (tk // LANES), axis=1)).astype(BF16)
            cols = slice(g * LANES, (g + 1) * LANES)
            if mode == "diff":
                vaug = jnp.concatenate([v_ref[0, pl.ds(ks, tk), cols], jnp.ones((tk, LANES), BF16)], axis=1)
                acc_s[g] = jnp.concatenate([alpha, alpha], axis=1) * acc_s[g] + _dot(p, vaug)
            else:
                pv = jnp.concatenate([_dot(p[:tq], vlo_ref[0, pl.ds(ks, tk), cols]),
                                      _dot(p[tq:], vhi_ref[0, pl.ds(ks, tk), cols])], axis=0)
                acc_s[g] = alpha * acc_s[g] + pv
            m_s[g] = m_new

    def body(j, carry):
        softmax_pv(j, scores(j), False)
        return carry

    lax.fori_loop(0, n_full, body, 0)
    softmax_pv(n_full, scores(n_full), True)

    if mode == "diff":
        lam = (jnp.exp(jnp.sum(lam_ref[0:1, :] * lam_ref[1:2, :], axis=-1, keepdims=True))
               - jnp.exp(jnp.sum(lam_ref[2:3, :] * lam_ref[3:4, :], axis=-1, keepdims=True)) + lam_init)
    for g in range(4):
        acc = acc_s[g]
        if mode == "diff":
            o = acc[:, :LANES] / acc[:, LANES:]
            out = _rms_norm(o[:tq] - lam * o[tq:], g_ref[...]) * (1.0 - lam_init)
        else:
            a_lo, a_hi = acc[:tq], acc[tq:]
            out = jnp.where(lo, a_lo / pltpu.roll(a_lo, 64, axis=1), a_hi / pltpu.roll(a_hi, 64, axis=1))
        o_ref[0, :, g * LANES:(g + 1) * LANES] = out.astype(BF16)


def _attention(mode, q, k, vs, extra, *, tq, tk, q_off, lam_init=0.0):
    b, t_q, qw = q.shape
    t_k = k.shape[1]
    assert t_q % tq == 0 and t_k % tk == 0
    assert all((q_off + i * tq) // tk == (q_off + (i + 1) * tq - 1) // tk for i in range(t_q // tq))
    in_specs = [pl.BlockSpec((1, tq, qw), lambda bi, qi: (bi, qi, 0)),
                pl.BlockSpec((1, t_k, qw), lambda bi, qi: (bi, 0, 0))]
    in_specs += [pl.BlockSpec((1, t_k, 4 * LANES), lambda bi, qi: (bi, 0, 0))] * len(vs)
    if mode == "fox":
        in_specs += [pl.BlockSpec((1, tq, A_HEADS), lambda bi, qi: (bi, qi, 0)),
                     pl.BlockSpec((1, t_k // tk, A_HEADS, tk), lambda bi, qi: (bi, 0, 0, 0))]
    elif mode == "diff":
        in_specs += [_const_spec((4, B_HD)), _const_spec((1, 2 * B_HD))]
    acc_w = 2 * LANES if mode == "diff" else LANES
    return pl.pallas_call(
        functools.partial(_attn_kernel, mode=mode, tq=tq, tk=tk, q_off=q_off, lam_init=lam_init),
        grid=(b, t_q // tq),
        in_specs=in_specs,
        out_specs=pl.BlockSpec((1, tq, 4 * LANES), lambda bi, qi: (bi, qi, 0)),
        out_shape=jax.ShapeDtypeStruct((b, t_q, 4 * LANES), BF16),
        scratch_shapes=[pltpu.VMEM((4, 2 * tq, LANES), F32), pltpu.VMEM((4, 2 * tq, acc_w), F32)],
        compiler_params=_cparams(2),
        name=f"attn_{mode}",
    )(q, k, *vs, *extra)


def _merge_kernel(h_ref, oa_ref, ob_ref, oc_ref, wg_ref, bg_ref, wb_ref, wo_ref, g_ref, b_ref, o_ref):
    h = h_ref[...]
    hb = h.astype(BF16)
    d = h.shape[1]
    merged = None
    for br, o_br in enumerate((oa_ref, ob_ref, oc_ref)):
        gate = jax.nn.sigmoid(_dot(hb, wg_ref[:, br * d:(br + 1) * d]) + bg_ref[:, br * d:(br + 1) * d])
        term = gate * _dot(o_br[...], wb_ref[br])
        merged = term if merged is None else merged + term
    mix = _dot(merged.astype(BF16), wo_ref[...])
    o_ref[...] = _layer_norm(ALPHA * h + mix, g_ref[...], b_ref[...])


def _merge(h, oa, ob, oc, w, layer, tm=512):
    n, d = h.shape
    tm = min(tm, n)
    row = lambda w_: pl.BlockSpec((tm, w_), lambda i: (i, 0))
    return pl.pallas_call(
        _merge_kernel,
        grid=(n // tm,),
        in_specs=[row(d), row(BRANCH_W), row(BRANCH_W), row(BRANCH_W),
                  _const_spec((d, 3 * d), layer), _const_spec((1, 3 * d)),
                  _const_spec((3, BRANCH_W, d), layer), _const_spec((d, d), layer),
                  _const_spec((1, d)), _const_spec((1, d))],
        out_specs=row(d),
        out_shape=jax.ShapeDtypeStruct((n, d), F32),
        compiler_params=_cparams(1),
        name="merge",
    )(h, oa, ob, oc, w["w_gate"], w["b_gate"][layer], w["w_branch"], w["w_out"],
      w["ln_g"][layer][1], w["ln_b"][layer][1])


def _rope_table(pos, groups, rows):
    t = pos.shape[0]
    one, zero = jnp.ones((t, 1), F32), jnp.zeros((t, 1), F32)
    c_cols, s1_cols, s2_cols = [], [], []
    lane = 0
    for start, half, theta in groups:
        inv = jnp.exp(-math.log(theta) * jnp.arange(half, dtype=F32) * (2.0 / (2 * half)))
        ang = pos.astype(F32)[:, None] * inv[None, :]
        cos, sin = jnp.cos(ang), jnp.sin(ang)
        gap = start - lane
        c_cols += [jnp.tile(one, (1, gap)), cos, cos]
        s1_cols += [jnp.tile(zero, (1, gap)), -sin, jnp.zeros_like(sin)]
        s2_cols += [jnp.tile(zero, (1, gap)), jnp.zeros_like(sin), sin]
        lane = start + 2 * half
    gap = LANES - lane
    tabs = [jnp.concatenate(cols + [jnp.tile(fill, (1, gap))], axis=1)
            for cols, fill in ((c_cols, one), (s1_cols, zero), (s2_cols, zero))]
    tab = jnp.stack(tabs, axis=0)
    return jnp.tile(tab, (1, rows // t, 1)) if rows > t else tab


def _rope_tables(pos, rows):
    h_b, h_c = B_ROT // 2, C_ROPE // 2
    return (_rope_table(pos, [(0, h_b, ROPE_THETA), (B_HD, h_b, ROPE_THETA)], rows),
            _rope_table(pos, [(C_NOPE, h_c, MLA_ROPE_THETA)], rows),
            _rope_table(pos, [(0, h_c, MLA_ROPE_THETA), (C_NOPE, h_c, MLA_ROPE_THETA)], rows))


def _prepare_weights(ffn1_w_in, ffn1_w_out, ffn2_w_in, ffn2_w_out, ln_g, ln_b, w_in_mix, b_forget,
                     diff_lambda, diff_norm_g, mla_q_norm_g, mla_w_uq, mla_kv_norm_g, mla_w_ukv,
                     w_branch, w_gate, b_gate, w_out, ple_w_gate, ple_b_gate, ple_w_proj):
    depth, d = w_in_mix.shape[0], w_in_mix.shape[1]

    offs = [sum(MIX_SPLITS[:i]) for i in range(1, len(MIX_SPLITS))]
    qa, ka, va, fa, qb, kb, vb, dq, dkv, kr = jnp.split(w_in_mix, offs, axis=-1)
    z = lambda n_: jnp.zeros((depth, d, n_), F32)
    w_mix = jnp.concatenate([qa * (A_HD ** -0.5), ka, va, qb * (B_HD ** -0.5), kb, vb, dq, dkv,
                             fa, z(LANES - A_HEADS), kr, z(C_NOPE - C_ROPE), kr, z(LANES - C_NOPE - C_ROPE)],
                            axis=-1).astype(BF16)
    w_uq = mla_w_uq.reshape(depth, C_Q_LORA, C_HEADS, C_NOPE + C_ROPE)
    w_uq = jnp.pad(w_uq, ((0, 0), (0, 0), (0, 0), (0, LANES - C_NOPE - C_ROPE)))
    w_uq = w_uq.reshape(depth, C_Q_LORA, C_HEADS * LANES).astype(BF16)
    w_uk = jnp.pad(mla_w_ukv[..., :C_NOPE], ((0, 0), (0, 0), (0, 0), (0, LANES - C_NOPE)))
    w_uv = mla_w_ukv[..., C_NOPE:]
    w_ukv = jnp.concatenate([w_uk.reshape(depth, C_KV_LORA, C_HEADS * LANES),
                             w_uv.reshape(depth, C_KV_LORA, C_HEADS * C_VD)], axis=-1).astype(BF16)
    vec = lambda a: a[:, None, :]
    return dict(
        ffn1_w_in=ffn1_w_in.astype(BF16), ffn1_w_out=ffn1_w_out.astype(BF16),
        ffn2_w_in=ffn2_w_in.astype(BF16), ffn2_w_out=ffn2_w_out.astype(BF16),
        ln_g=ln_g[:, :, None, :], ln_b=ln_b[:, :, None, :],
        w_mix=w_mix, b_forget=vec(b_forget), diff_lambda=diff_lambda, diff_norm_g=vec(diff_norm_g),
        mla_q_norm_g=vec(mla_q_norm_g), mla_kv_norm_g=vec(mla_kv_norm_g), mla_w_uq=w_uq, mla_w_ukv=w_ukv,
        w_branch=w_branch.astype(BF16), w_gate=w_gate.astype(BF16), b_gate=vec(b_gate),
        w_out=w_out.astype(BF16), ple_w_gate=ple_w_gate.astype(BF16), ple_b_gate=vec(ple_b_gate),
        ple_w_proj=ple_w_proj.astype(BF16))


def _layer_step(x, p, layer, w, tabs, cache, past, states):
    b, t, d = x.shape
    n = b * t
    lam_init = 0.8 - 0.6 * math.exp(-0.3 * layer)
    x1 = _ffn_ln(x.reshape(n, d), w, layer, 1)
    (qa, ka, kab, va, valo, vahi, logf, qb, kb, kbb, vb, vbb, ckv, kpe, qm, km, vmlo, vmhi) = _mix_proj(
        x1, w, layer, tabs, states)
    states = (ka, va, logf, kb, vb, ckv, kpe)
    r3 = lambda a: a.reshape(b, t, a.shape[-1])
    qa, kab, valo, vahi, qb, kbb, vbb, qm, km, vmlo, vmhi = map(
        r3, (qa, kab, valo, vahi, qb, kbb, vbb, qm, km, vmlo, vmhi))
    logf_t = r3(logf[layer]).transpose(0, 2, 1)
    if cache is None:
        tq = tk = min(256, t)
        q_off = 0
        cum = _cumsum_time(logf_t)
        fq_src = cum
    else:
        c_ka, c_va, c_logf, c_kb, c_vb, c_ckv, c_kpe = cache
        t_k = -(-(past + t) // LANES) * LANES
        pad = t_k - past - t
        tq, tk, q_off = t, t_k, past

        def with_cache(c, new):
            c = c.reshape(b, past, -1).astype(BF16)
            return jnp.concatenate([c, new, jnp.zeros((b, pad, new.shape[-1]), BF16)], axis=1)

        c_va = c_va.reshape(b, past, -1)
        even_head = (jnp.arange(c_va.shape[-1]) % LANES) < 64
        kab, kbb, vbb = (with_cache(c, a) for c, a in ((c_ka, kab), (c_kb, kbb), (c_vb, vbb)))
        valo = with_cache(jnp.where(even_head, c_va, 1.0), valo)
        vahi = with_cache(jnp.where(even_head, 1.0, c_va), vahi)
        kpe_wide = jnp.pad(c_kpe.reshape(b * past, C_ROPE), ((0, 0), (C_NOPE, LANES - C_NOPE - C_ROPE)))
        km_c, vmlo_c, vmhi_c = _mla_cache_proj(c_ckv.reshape(b * past, C_KV_LORA), kpe_wide, w, layer)
        km, vmlo, vmhi = (with_cache(c, a) for c, a in ((km_c, km), (vmlo_c, vmlo), (vmhi_c, vmhi)))
        logf_all = jnp.concatenate([c_logf.astype(F32).transpose(0, 2, 1), logf_t,
                                    jnp.zeros((b, A_HEADS, pad), F32)], axis=2)
        cum = _cumsum_time(logf_all)
        fq_src = cum[:, :, past:past + t]
    fq = fq_src.transpose(0, 2, 1)
    fk = cum.reshape(b, A_HEADS, cum.shape[2] // tk, tk).transpose(0, 2, 1, 3)
    oa = _attention("fox", qa, kab, (valo, vahi), (fq, fk), tq=tq, tk=tk, q_off=q_off)
    ob = _attention("diff", qb, kbb, (vbb,), (w["diff_lambda"][layer], w["diff_norm_g"][layer]),
                    tq=tq, tk=tk, q_off=q_off, lam_init=lam_init)
    oc = _attention("mla", qm, km, (vmlo, vmhi), (), tq=tq, tk=tk, q_off=q_off)
    f2 = lambda a: a.reshape(n, a.shape[-1])
    x2 = _merge(x1, f2(oa), f2(ob), f2(oc), w, layer)
    x3 = _ffn_ln(x2, w, layer, 2, p=p.reshape(n, PLE_DIM))
    return x3.reshape(b, t, d), states


def kernel(x_prompt, x_sample, p_prompt, p_sample, cache_fox_k, cache_fox_v, cache_fox_logf, cache_diff_k, cache_diff_v, cache_mla_ckv, cache_mla_kpe, ffn1_w_in, ffn1_w_out, ffn2_w_in, ffn2_w_out, ln_g, ln_b, w_in_mix, b_forget, diff_lambda, diff_norm_g, mla_q_norm_g, mla_w_uq, mla_kv_norm_g, mla_w_ukv, w_branch, w_gate, b_gate, w_out, ple_w_gate, ple_b_gate, ple_w_proj):
    w = _prepare_weights(ffn1_w_in, ffn1_w_out, ffn2_w_in, ffn2_w_out, ln_g, ln_b, w_in_mix, b_forget,
                         diff_lambda, diff_norm_g, mla_q_norm_g, mla_w_uq, mla_kv_norm_g, mla_w_ukv,
                         w_branch, w_gate, b_gate, w_out, ple_w_gate, ple_b_gate, ple_w_proj)
    depth = w_in_mix.shape[0]
    t_p, t_s = x_prompt.shape[1], x_sample.shape[1]
    past = cache_fox_k.shape[2]
    n_s = x_sample.shape[0] * t_s
    tabs_p = _rope_tables(jnp.arange(t_p, dtype=jnp.int32), t_p)
    tabs_s = _rope_tables(past + jnp.arange(t_s, dtype=jnp.int32), min(512, n_s))

    xp, xs = x_prompt, x_sample
    st_p = st_s = None
    for i in range(depth):
        xp, st_p = _layer_step(xp, p_prompt[i], i, w, tabs_p, None, 0, st_p)
        cache_i = (cache_fox_k[i], cache_fox_v[i], cache_fox_logf[i], cache_diff_k[i], cache_diff_v[i],
                   cache_mla_ckv[i], cache_mla_kpe[i])
        xs, st_s = _layer_step(xs, p_sample[i], i, w, tabs_s, cache_i, past, st_s)

    def shaped(states, x):
        lead = (depth,) + x.shape[:2]
        tails = ((A_HEADS, A_HD), (A_HEADS, A_HD), (A_HEADS,), (B_HEADS, 2, B_HD), (B_HEADS, 2 * B_HD),
                 (C_KV_LORA,), (C_ROPE,))
        return [s.reshape(lead + tail) for s, tail in zip(states, tails)]

    sp, ss = shaped(st_p, x_prompt), shaped(st_s, x_sample)
    return (xp, xs, sp[0], ss[0], sp[1], ss[1], sp[2], ss[2], sp[3], ss[3], sp[4], ss[4],
            sp[5], ss[5], sp[6], ss[6])
```
